```python
import math
import jax, jax.numpy as jnp
from jax import lax
import numpy as np

D_MODEL = 4096
BATCH = 4
SEQ = 4096
DEPTH = 1

HEAD_DIM = 128
ATTN_WIDTH = D_MODEL // 2
CONV_WIDTH = D_MODEL - ATTN_WIDTH
MIX_WIDTH = ATTN_WIDTH + CONV_WIDTH
N_Q_HEADS = ATTN_WIDTH // HEAD_DIM
N_KV_HEADS = max(1, N_Q_HEADS // 4)
GQA_GROUP = N_Q_HEADS // N_KV_HEADS
KV_WIDTH = N_KV_HEADS * HEAD_DIM
WINDOW = 128
ATTN_BLOCK = 128
ROPE_THETA = 500000.0
ROPE_DIM = HEAD_DIM // 4
CONV_WIDTH_T = 31
NORM_EPS = 1e-6
LN_EPS = 1e-5
IN_SIZES = (ATTN_WIDTH, KV_WIDTH, KV_WIDTH, ATTN_WIDTH, 2 * CONV_WIDTH, CONV_WIDTH)
IN_WIDTH = sum(IN_SIZES)
SPLITS = tuple(int(v) for v in np.cumsum(IN_SIZES)[:-1])

kernel_name = "hymba_swa_conformer_conv_encoder_layer"


def rms_norm(x, gain, eps):
    x32 = x.astype(jnp.float32)
    y = x32 * lax.rsqrt(jnp.mean(x32 * x32, axis=-1, keepdims=True) + eps)
    return (y * gain.astype(jnp.float32)).astype(x.dtype)


def layer_norm(x, gain, bias, eps):
    x32 = x.astype(jnp.float32)
    mu = jnp.mean(x32, axis=-1, keepdims=True)
    xc = x32 - mu
    var = jnp.mean(xc * xc, axis=-1, keepdims=True)
    y = xc * lax.rsqrt(var + eps) * gain.astype(jnp.float32) + bias.astype(jnp.float32)
    return y.astype(x.dtype)


def partial_rope(x, pos):
    half = ROPE_DIM // 2
    inv_freq = jnp.power(jnp.float32(ROPE_THETA), -jnp.arange(half, dtype=jnp.float32) * (2.0 / ROPE_DIM))
    ang = pos.astype(jnp.float32)[:, None] * inv_freq[None, :]
    cos = jnp.cos(ang)[:, None, :]
    sin = jnp.sin(ang)[:, None, :]
    x32 = x.astype(jnp.float32)
    x1, x2, rest = x32[..., :half], x32[..., half:ROPE_DIM], x32[..., ROPE_DIM:]
    out = jnp.concatenate([x1 * cos - x2 * sin, x2 * cos + x1 * sin, rest], axis=-1)
    return out.astype(x.dtype)


def banded_window_attention(q, k, v, sink):
    B, S = q.shape[0], q.shape[1]
    nb = S // ATTN_BLOCK
    qb = q.reshape(B, nb, ATTN_BLOCK, N_KV_HEADS, GQA_GROUP, HEAD_DIM)

    def neighbours(t):
        tp = jnp.pad(t, ((0, 0), (ATTN_BLOCK, ATTN_BLOCK), (0, 0), (0, 0)))
        tb = tp.reshape(B, nb + 2, ATTN_BLOCK, N_KV_HEADS, HEAD_DIM)
        return jnp.concatenate([tb[:, :-2], tb[:, 1:-1], tb[:, 2:]], axis=2)

    kb, vb = neighbours(k), neighbours(v)
    scale = 1.0 / math.sqrt(HEAD_DIM)
    scores = jnp.einsum('bnqkgd,bnskd->bnkgqs', qb, kb,
                        preferred_element_type=jnp.float32) * scale
    blk_idx = jnp.arange(nb)[:, None] * ATTN_BLOCK
    q_pos = blk_idx + jnp.arange(ATTN_BLOCK)[None, :]
    k_pos = blk_idx - ATTN_BLOCK + jnp.arange(3 * ATTN_BLOCK)[None, :]
    rel = k_pos[:, None, :] - q_pos[:, :, None]
    valid = (jnp.abs(rel) <= WINDOW) & (k_pos[:, None, :] >= 0) & (k_pos[:, None, :] < S)
    scores = jnp.where(valid[None, :, None, None, :, :], scores, -jnp.inf)
    sink_b = sink.astype(jnp.float32).reshape(1, 1, N_KV_HEADS, GQA_GROUP, 1, 1)
    m = jnp.maximum(jnp.max(scores, axis=-1, keepdims=True), sink_b)
    p = jnp.exp(scores - m)
    denom = jnp.sum(p, axis=-1, keepdims=True) + jnp.exp(sink_b - m)
    probs = (p / denom).astype(v.dtype)
    out = jnp.einsum('bnkgqs,bnskd->bnqkgd', probs, vb)
    return out.reshape(B, S, N_Q_HEADS * HEAD_DIM)


def conformer_conv(u, dw_w, dw_b, ln_g, ln_b, pw_w, pw_b):
    a, b = jnp.split(u, 2, axis=-1)
    h = a * jax.nn.sigmoid(b)
    pad = CONV_WIDTH_T // 2
    h = lax.conv_general_dilated(h, dw_w[:, None, :], window_strides=(1,),
                                 padding=[(pad, pad)],
                                 dimension_numbers=('NWC', 'WIO', 'NWC'),
                                 feature_group_count=CONV_WIDTH) + dw_b
    h = jax.nn.silu(layer_norm(h, ln_g, ln_b, LN_EPS))
    return jnp.einsum('bsc,ce->bse', h, pw_w) + pw_b


def setup_inputs(seed: int = 0) -> dict:
    key = jax.random.key(seed)
    ks = jax.random.split(key, 14)
    f32 = jnp.float32
    nrm = lambda k, shape: jax.random.normal(k, shape, f32)
    return {
        "x": nrm(ks[0], (BATCH, SEQ, D_MODEL)),
        "norm_gain": 1.0 + 0.02 * nrm(ks[1], (DEPTH, D_MODEL)),
        "w_in": nrm(ks[2], (DEPTH, D_MODEL, IN_WIDTH)) * D_MODEL ** -0.5,
        "q_norm_gain": 1.0 + 0.02 * nrm(ks[3], (DEPTH, HEAD_DIM)),
        "k_norm_gain": 1.0 + 0.02 * nrm(ks[4], (DEPTH, HEAD_DIM)),
        "attn_sink": 0.5 * nrm(ks[5], (DEPTH, N_Q_HEADS)),
        "conv_dw_w": nrm(ks[6], (DEPTH, CONV_WIDTH_T, CONV_WIDTH)) * CONV_WIDTH_T ** -0.5,
        "conv_dw_b": 0.02 * nrm(ks[7], (DEPTH, CONV_WIDTH)),
        "conv_ln_gain": 1.0 + 0.02 * nrm(ks[8], (DEPTH, CONV_WIDTH)),
        "conv_ln_bias": 0.02 * nrm(ks[9], (DEPTH, CONV_WIDTH)),
        "conv_pw_w": nrm(ks[10], (DEPTH, CONV_WIDTH, CONV_WIDTH)) * CONV_WIDTH ** -0.5,
        "conv_pw_b": 0.02 * nrm(ks[11], (DEPTH, CONV_WIDTH)),
        "w_out": nrm(ks[12], (DEPTH, MIX_WIDTH, D_MODEL)) * MIX_WIDTH ** -0.5,
    }


def reference(x, norm_gain, w_in, q_norm_gain, k_norm_gain, attn_sink, conv_dw_w, conv_dw_b,
              conv_ln_gain, conv_ln_bias, conv_pw_w, conv_pw_b, w_out):
    B, S = x.shape[0], x.shape[1]
    pos = jnp.arange(S)
    for layer in range(DEPTH):
        xn = rms_norm(x, norm_gain[layer], NORM_EPS)
        proj = jnp.einsum('bsd,de->bse', xn, w_in[layer])
        q, k, v, g_attn, u_conv, g_conv = jnp.split(proj, SPLITS, axis=-1)
        q = q.reshape(B, S, N_Q_HEADS, HEAD_DIM)
        k = k.reshape(B, S, N_KV_HEADS, HEAD_DIM)
        v = v.reshape(B, S, N_KV_HEADS, HEAD_DIM)
        q = partial_rope(rms_norm(q, q_norm_gain[layer], NORM_EPS), pos)
        k = partial_rope(rms_norm(k, k_norm_gain[layer], NORM_EPS), pos)
        attn = banded_window_attention(q, k, v, attn_sink[layer]) * jax.nn.silu(g_attn)
        conv = conformer_conv(u_conv, conv_dw_w[layer], conv_dw_b[layer], conv_ln_gain[layer],
                              conv_ln_bias[layer], conv_pw_w[layer], conv_pw_b[layer]) * jax.nn.silu(g_conv)
        mixed = jnp.concatenate([attn, conv], axis=-1)
        x = x + jnp.einsum('bsm,md->bsd', mixed, w_out[layer])
    return x
```

```python
import functools
import math

import jax
import jax.numpy as jnp
from jax import lax
from jax.experimental import pallas as pl
from jax.experimental.pallas import tpu as pltpu

F32 = jnp.float32
BF16 = jnp.bfloat16

HEAD_DIM = 128
N_Q_HEADS = 16
N_KV_HEADS = 4
GQA_GROUP = N_Q_HEADS // N_KV_HEADS
ATTN_BLOCK = 128
WINDOW = 128
ROPE_THETA = 500000.0
ROPE_DIM = HEAD_DIM // 4
ROPE_HALF = ROPE_DIM // 2
CONV_TAPS = 31
CONV_PAD = CONV_TAPS // 2
NORM_EPS = 1e-6
LN_EPS = 1e-5
MASK_VALUE = -1e30

LANES = 128
BF16_SUBLANE_TILE = 16
VMEM_LIMIT_BYTES = 56 * 1024 * 1024


def _params(*sem):
    return pltpu.CompilerParams(dimension_semantics=sem, vmem_limit_bytes=VMEM_LIMIT_BYTES)


def _rmsnorm_kernel(x_ref, g_ref, o_ref):
    x = x_ref[...]
    ms = jnp.mean(x * x, axis=-1, keepdims=True)
    o_ref[...] = (x * lax.rsqrt(ms + NORM_EPS) * g_ref[...]).astype(o_ref.dtype)


def _rmsnorm(x2d, gain, tm=256):
    t, d = x2d.shape
    return pl.pallas_call(
        _rmsnorm_kernel,
        grid=(t // tm,),
        in_specs=[pl.BlockSpec((tm, d), lambda i: (i, 0)),
                  pl.BlockSpec((1, d), lambda i: (0, 0))],
        out_specs=pl.BlockSpec((tm, d), lambda i: (i, 0)),
        out_shape=jax.ShapeDtypeStruct((t, d), BF16),
        compiler_params=_params("parallel"),
        name="rmsnorm",
    )(x2d, gain.reshape(1, d))


def _qkv_kernel(xn_ref, w_ref, cos_ref, sa_ref, sb_ref, qg_ref, kg_ref, o_ref, *, n_q_tiles):
    j = pl.program_id(1)
    acc = jnp.dot(xn_ref[...], w_ref[...], preferred_element_type=F32)
    tn = acc.shape[1]

    @pl.when(j <= n_q_tiles)
    def _():
        is_q = j < n_q_tiles
        gain = jnp.where(is_q, qg_ref[...], kg_ref[...])
        cos, sa, sb = cos_ref[...], sa_ref[...], sb_ref[...]
        for h in range(tn // HEAD_DIM):
            xh = acc[:, h * HEAD_DIM:(h + 1) * HEAD_DIM]
            ms = jnp.mean(xh * xh, axis=-1, keepdims=True)
            y = xh * lax.rsqrt(ms + NORM_EPS) * gain
            y = (y * cos + pltpu.roll(y, HEAD_DIM - ROPE_HALF, 1) * sa
                 + pltpu.roll(y, ROPE_HALF, 1) * sb)
            o_ref[:, h * HEAD_DIM:(h + 1) * HEAD_DIM] = y.astype(o_ref.dtype)

    @pl.when(j > n_q_tiles)
    def _():
        o_ref[...] = acc.astype(o_ref.dtype)


def _silu_kernel(xn_ref, w_ref, o_ref):
    acc = jnp.dot(xn_ref[...], w_ref[...], preferred_element_type=F32)
    o_ref[...] = (acc * jax.nn.sigmoid(acc)).astype(o_ref.dtype)


def _glu_kernel(xn_ref, w_ref, o_ref):
    acc = jnp.dot(xn_ref[...], w_ref[...], preferred_element_type=F32)
    half = acc.shape[1] // 2
    o_ref[...] = (acc[:, :half] * jax.nn.sigmoid(acc[:, half:])).astype(o_ref.dtype)


def _proj_call(kernel, xn, w, extra_inputs, extra_specs, tm, tn, out_cols, out_tn, name):
    t, d = xn.shape
    n = w.shape[1]
    return pl.pallas_call(
        kernel,
        grid=(t // tm, n // tn),
        in_specs=[pl.BlockSpec((tm, d), lambda i, j: (i, 0)),
                  pl.BlockSpec((d, tn), lambda i, j: (0, j))] + extra_specs,
        out_specs=pl.BlockSpec((tm, out_tn), lambda i, j: (i, j)),
        out_shape=jax.ShapeDtypeStruct((t, out_cols), BF16),
        compiler_params=_params("parallel", "arbitrary"),
        name=name,
    )(xn, w, *extra_inputs)


def _rope_tables(seq, q_scale_unused=None):
    inv_freq = jnp.power(jnp.float32(ROPE_THETA),
                         -jnp.arange(ROPE_HALF, dtype=F32) * (2.0 / ROPE_DIM))
    ang = jnp.arange(seq).astype(F32)[:, None] * inv_freq[None, :]
    cos, sin = jnp.cos(ang), jnp.sin(ang)
    zeros = jnp.zeros((seq, HEAD_DIM - ROPE_DIM), F32)
    zh = jnp.zeros((seq, ROPE_HALF), F32)
    cos_t = jnp.concatenate([cos, cos, jnp.ones_like(zeros)], axis=-1)
    sa_t = jnp.concatenate([-sin, zh, zeros], axis=-1)
    sb_t = jnp.concatenate([zh, sin, zeros], axis=-1)
    return cos_t, sa_t, sb_t


def _attn_kernel(sink_ref, q_ref, kp_ref, kc_ref, kn_ref, vp_ref, vc_ref, vn_ref, g_ref, o_ref,
                 *, n_blocks):
    n = pl.program_id(1)
    blk = ATTN_BLOCK
    kk = jnp.concatenate([kp_ref[...], kc_ref[...], kn_ref[...]], axis=0)
    vv = jnp.concatenate([vp_ref[...], vc_ref[...], vn_ref[...]], axis=0)
    row = lax.broadcasted_iota(jnp.int32, (blk, 3 * blk), 0)
    col = lax.broadcasted_iota(jnp.int32, (blk, 3 * blk), 1)
    rel = col - blk - row
    valid = (jnp.abs(rel) <= WINDOW)
    valid = valid & ((col >= blk) | (n > 0)) & ((col < 2 * blk) | (n < n_blocks - 1))
    bias1 = jnp.where(valid, 0.0, MASK_VALUE).astype(F32)
    bias = jnp.concatenate([bias1] * GQA_GROUP, axis=0)
    for h in range(N_KV_HEADS):
        qs = jnp.concatenate(
            [q_ref[:, (h * GQA_GROUP + g) * HEAD_DIM:(h * GQA_GROUP + g + 1) * HEAD_DIM]
             for g in range(GQA_GROUP)], axis=0)
        kh = kk[:, h * HEAD_DIM:(h + 1) * HEAD_DIM]
        vh = vv[:, h * HEAD_DIM:(h + 1) * HEAD_DIM]
        s = lax.dot_general(qs, kh, (((1,), (1,)), ((), ())), preferred_element_type=F32) + bias
        sink = jnp.concatenate(
            [jnp.full((blk, 1), sink_ref[h * GQA_GROUP + g], F32) for g in range(GQA_GROUP)], axis=0)
        m = jnp.maximum(jnp.max(s, axis=-1, keepdims=True), sink)
        p = jnp.exp(s - m)
        den = jnp.sum(p, axis=-1, keepdims=True) + jnp.exp(sink - m)
        o = jnp.dot(p.astype(BF16), vh, preferred_element_type=F32) / den
        for g in range(GQA_GROUP):
            c0 = (h * GQA_GROUP + g) * HEAD_DIM
            gate = g_ref[:, c0:c0 + HEAD_DIM].astype(F32)
            o_ref[:, c0:c0 + HEAD_DIM] = (o[g * blk:(g + 1) * blk] * gate).astype(o_ref.dtype)


def _attention(qkv, gates, sink, batch, seq):
    t = qkv.shape[0]
    blk = ATTN_BLOCK
    nb = seq // blk
    qw = N_Q_HEADS * HEAD_DIM
    kvw = N_KV_HEADS * HEAD_DIM
    k_col = qw // kvw
    v_col = k_col + 1

    def kv_spec(col, shift):
        def imap(b, n):
            return (b * nb + jnp.clip(n + shift, 0, nb - 1), col)
        return pl.BlockSpec((blk, kvw), imap)

    return pl.pallas_call(
        functools.partial(_attn_kernel, n_blocks=nb),
        grid=(batch, nb),
        in_specs=[pl.BlockSpec(memory_space=pltpu.SMEM),
                  pl.BlockSpec((blk, qw), lambda b, n: (b * nb + n, 0)),
                  kv_spec(k_col, -1), kv_spec(k_col, 0), kv_spec(k_col, 1),
                  kv_spec(v_col, -1), kv_spec(v_col, 0), kv_spec(v_col, 1),
                  pl.BlockSpec((blk, qw), lambda b, n: (b * nb + n, 0))],
        out_specs=pl.BlockSpec((blk, qw), lambda b, n: (b * nb + n, 0)),
        out_shape=jax.ShapeDtypeStruct((t, qw), BF16),
        compiler_params=_params("parallel", "arbitrary"),
        name="band_attention",
    )(sink, qkv, qkv, qkv, qkv, qkv, qkv, qkv, gates)


def _conv_kernel(hp_ref, hc_ref, hn_ref, dww_ref, dwb_ref, lng_ref, lnb_ref, pww_ref, pwb_ref,
                 g_ref, o_ref, hbuf, cbuf, *, n_tiles):
    i = pl.program_id(1)
    ts = hc_ref.shape[0]
    halo = BF16_SUBLANE_TILE
    width = hc_ref.shape[1]
    hbuf[0:halo, :] = jnp.where(i > 0, hp_ref[...].astype(F32), 0.0)
    hbuf[halo:halo + ts, :] = hc_ref[...].astype(F32)
    hbuf[halo + ts:, :] = jnp.where(i < n_tiles - 1, hn_ref[...].astype(F32), 0.0)
    base = halo - CONV_PAD
    for c in range(width // LANES):
        cs = slice(c * LANES, (c + 1) * LANES)
        acc = jnp.broadcast_to(dwb_ref[:, cs], (ts, LANES))
        for tap in range(CONV_TAPS):
            acc = acc + hbuf[base + tap:base + tap + ts, cs] * dww_ref[tap:tap + 1, cs]
        cbuf[:, cs] = acc
    y = cbuf[...]
    mu = jnp.mean(y, axis=-1, keepdims=True)
    yc = y - mu
    var = jnp.mean(yc * yc, axis=-1, keepdims=True)
    yn = yc * lax.rsqrt(var + LN_EPS) * lng_ref[...] + lnb_ref[...]
    z = (yn * jax.nn.sigmoid(yn)).astype(BF16)
    o = jnp.dot(z, pww_ref[...], preferred_element_type=F32) + pwb_ref[...]
    o_ref[...] = (o * g_ref[...].astype(F32)).astype(o_ref.dtype)


def _conformer_conv(hglu, gates, dw_w, dw_b, ln_g, ln_b, pw_w, pw_b, batch, seq, ts=256):
    t, width = hglu.shape
    nt = seq // ts
    halo = BF16_SUBLANE_TILE
    r = ts // halo
    nh = seq // halo
    row = lambda v: v.reshape(1, width)
    const = lambda b, i: (0, 0)
    return pl.pallas_call(
        functools.partial(_conv_kernel, n_tiles=nt),
        grid=(batch, nt),
        in_specs=[pl.BlockSpec((halo, width), lambda b, i: (b * nh + jnp.maximum(i * r - 1, 0), 0)),
                  pl.BlockSpec((ts, width), lambda b, i: (b * nt + i, 0)),
                  pl.BlockSpec((halo, width), lambda b, i: (b * nh + jnp.minimum((i + 1) * r, nh - 1), 0)),
                  pl.BlockSpec((CONV_TAPS, width), const),
                  pl.BlockSpec((1, width), const),
                  pl.BlockSpec((1, width), const),
                  pl.BlockSpec((1, width), const),
                  pl.BlockSpec((width, width), const),
                  pl.BlockSpec((1, width), const),
                  pl.BlockSpec((ts, width), lambda b, i: (b * nt + i, 1))],
        out_specs=pl.BlockSpec((ts, width), lambda b, i: (b * nt + i, 0)),
        out_shape=jax.ShapeDtypeStruct((t, width), BF16),
        scratch_shapes=[pltpu.VMEM((ts + 2 * halo, width), F32),
                        pltpu.VMEM((ts, width), F32)],
        compiler_params=_params("parallel", "arbitrary"),
        name="conformer_conv",
    )(hglu, hglu, hglu, dw_w, row(dw_b), row(ln_g), row(ln_b), pw_w, row(pw_b), gates)


def _out_kernel(a_ref, c_ref, wa_ref, wc_ref, x_ref, o_ref):
    acc = jnp.dot(a_ref[...], wa_ref[...], preferred_element_type=F32)
    acc = acc + jnp.dot(c_ref[...], wc_ref[...], preferred_element_type=F32)
    o_ref[...] = x_ref[...] + acc


def _out_proj(attn, conv, w_a, w_c, x2d, tm=1024, tn=512):
    t, d = x2d.shape
    ka, kc = attn.shape[1], conv.shape[1]
    return pl.pallas_call(
        _out_kernel,
        grid=(t // tm, d // tn),
        in_specs=[pl.BlockSpec((tm, ka), lambda i, j: (i, 0)),
                  pl.BlockSpec((tm, kc), lambda i, j: (i, 0)),
                  pl.BlockSpec((ka, tn), lambda i, j: (0, j)),
                  pl.BlockSpec((kc, tn), lambda i, j: (0, j)),
                  pl.BlockSpec((tm, tn), lambda i, j: (i, j))],
        out_specs=pl.BlockSpec((tm, tn), lambda i, j: (i, j)),
        out_shape=jax.ShapeDtypeStruct((t, d), F32),
        compiler_params=_params("parallel", "arbitrary"),
        name="out_proj",
    )(attn, conv, w_a, w_c, x2d)


def _layer(x, norm_gain, w_in, q_gain, k_gain, sink, dw_w, dw_b, ln_g, ln_b, pw_w, pw_b, w_out):
    batch, seq, d = x.shape
    t = batch * seq
    qw = N_Q_HEADS * HEAD_DIM
    kvw = N_KV_HEADS * HEAD_DIM
    cw = dw_w.shape[1]
    x2d = x.reshape(t, d)

    c_q, c_k, c_v = 0, qw, qw + kvw
    c_ga = c_v + kvw
    c_ua = c_ga + qw
    c_ub = c_ua + cw
    c_gc = c_ub + cw
    w_bf = w_in.astype(BF16)
    w_qkv = w_bf[:, c_q:c_ga]
    w_gates = jnp.concatenate([w_bf[:, c_ga:c_ua], w_bf[:, c_gc:]], axis=1)
    glu_half = 512
    w_glu = jnp.concatenate(
        [w_bf[:, c_ua:c_ub].reshape(d, cw // glu_half, 1, glu_half),
         w_bf[:, c_ub:c_gc].reshape(d, cw // glu_half, 1, glu_half)], axis=2).reshape(d, 2 * cw)

    xn = _rmsnorm(x2d, norm_gain)

    tm = 1024
    cos_t, sa_t, sb_t = _rope_tables(seq)
    scale = 1.0 / math.sqrt(HEAD_DIM)
    qg = (q_gain.astype(F32) * scale).reshape(1, HEAD_DIM)
    kg = k_gain.astype(F32).reshape(1, HEAD_DIM)
    tn_qkv = kvw
    rope_spec = pl.BlockSpec((tm, HEAD_DIM), lambda i, j: (i % (seq // tm), 0))
    gain_spec = pl.BlockSpec((1, HEAD_DIM), lambda i, j: (0, 0))
    qkv = _proj_call(functools.partial(_qkv_kernel, n_q_tiles=qw // tn_qkv), xn, w_qkv,
                     [cos_t, sa_t, sb_t, qg, kg], [rope_spec] * 3 + [gain_spec] * 2,
                     tm, tn_qkv, qw + 2 * kvw, tn_qkv, "proj_qkv")
    gates = _proj_call(_silu_kernel, xn, w_gates, [], [], tm, 1024, qw + cw, 1024, "proj_gates")
    hglu = _proj_call(_glu_kernel, xn, w_glu, [], [], tm, 2 * glu_half, cw, glu_half, "proj_glu")

    attn = _attention(qkv, gates, sink.astype(F32), batch, seq)
    conv = _conformer_conv(hglu, gates, dw_w, dw_b, ln_g, ln_b, pw_w.astype(BF16), pw_b, batch, seq)

    w_o = w_out.astype(BF16)
    out = _out_proj(attn, conv, w_o[:qw], w_o[qw:], x2d)
    return out.reshape(batch, seq, d)


def kernel(x, norm_gain, w_in, q_norm_gain, k_norm_gain, attn_sink, conv_dw_w, conv_dw_b,
           conv_ln_gain, conv_ln_bias, conv_pw_w, conv_pw_b, w_out):
    depth = norm_gain.shape[0]
    for layer in range(depth):
        x = _layer(x, norm_gain[layer], w_in[layer], q_norm_gain[layer], k_norm_gain[layer],
                   attn_sink[layer], conv_dw_w[layer], conv_dw_b[layer], conv_ln_gain[layer],
                   conv_ln_bias[layer], conv_pw_w[layer], conv_pw_b[layer], w_out[layer])
    return x
```

```python
import functools
import math

import jax
import jax.numpy as jnp
from jax import lax
from jax.experimental import pallas as pl
from jax.experimental.pallas import tpu as pltpu

F32 = jnp.float32
BF16 = jnp.bfloat16

HEAD_DIM = 128
N_Q_HEADS = 16
N_KV_HEADS = 4
GQA_GROUP = N_Q_HEADS // N_KV_HEADS
ATTN_BLOCK = 128
WINDOW = 128
ROPE_THETA = 500000.0
ROPE_DIM = HEAD_DIM // 4
ROPE_HALF = ROPE_DIM // 2
CONV_TAPS = 31
CONV_PAD = CONV_TAPS // 2
NORM_EPS = 1e-6
LN_EPS = 1e-5
MASK_VALUE = -1e30

LANES = 128
SUBLANES = 8
CONV_ROW_BLOCK = 128
BF16_SUBLANE_TILE = 16
VMEM_LIMIT_BYTES = 56 * 1024 * 1024


def _params(*sem):
    return pltpu.CompilerParams(dimension_semantics=sem, vmem_limit_bytes=VMEM_LIMIT_BYTES)


def _rmsnorm_kernel(x_ref, g_ref, o_ref):
    x = x_ref[...]
    ms = jnp.mean(x * x, axis=-1, keepdims=True)
    o_ref[...] = (x * lax.rsqrt(ms + NORM_EPS) * g_ref[...]).astype(o_ref.dtype)


def _rmsnorm(x2d, gain, tm=256):
    t, d = x2d.shape
    return pl.pallas_call(
        _rmsnorm_kernel,
        grid=(t // tm,),
        in_specs=[pl.BlockSpec((tm, d), lambda i: (i, 0)),
                  pl.BlockSpec((1, d), lambda i: (0, 0))],
        out_specs=pl.BlockSpec((tm, d), lambda i: (i, 0)),
        out_shape=jax.ShapeDtypeStruct((t, d), BF16),
        compiler_params=_params("parallel"),
        name="rmsnorm",
    )(x2d, gain.reshape(1, d))


def _qkv_kernel(xn_ref, w_ref, cos_ref, sa_ref, sb_ref, qg_ref, kg_ref, o_ref, *, n_q_tiles):
    j = pl.program_id(1)
    acc = jnp.dot(xn_ref[...], w_ref[...], preferred_element_type=F32)
    tn = acc.shape[1]

    @pl.when(j <= n_q_tiles)
    def _():
        is_q = j < n_q_tiles
        gain = jnp.where(is_q, qg_ref[...], kg_ref[...])
        cos, sa, sb = cos_ref[...], sa_ref[...], sb_ref[...]
        for h in range(tn // HEAD_DIM):
            xh = acc[:, h * HEAD_DIM:(h + 1) * HEAD_DIM]
            ms = jnp.mean(xh * xh, axis=-1, keepdims=True)
            y = xh * lax.rsqrt(ms + NORM_EPS) * gain
            y = (y * cos + pltpu.roll(y, HEAD_DIM - ROPE_HALF, 1) * sa
                 + pltpu.roll(y, ROPE_HALF, 1) * sb)
            o_ref[:, h * HEAD_DIM:(h + 1) * HEAD_DIM] = y.astype(o_ref.dtype)

    @pl.when(j > n_q_tiles)
    def _():
        o_ref[...] = acc.astype(o_ref.dtype)


def _silu_kernel(xn_ref, w_ref, o_ref):
    acc = jnp.dot(xn_ref[...], w_ref[...], preferred_element_type=F32)
    o_ref[...] = (acc * jax.nn.sigmoid(acc)).astype(o_ref.dtype)


def _glu_kernel(xn_ref, wa_ref, wb_ref, o_ref):
    xn = xn_ref[...]
    a = jnp.dot(xn, wa_ref[...], preferred_element_type=F32)
    b = jnp.dot(xn, wb_ref[...], preferred_element_type=F32)
    o_ref[...] = (a * jax.nn.sigmoid(b)).astype(o_ref.dtype)


def _proj_call(kernel, xn, w, w_col_blocks, extra_inputs, extra_specs, tm, tn, n_tiles, name):
    t, d = xn.shape
    w_specs = [pl.BlockSpec((d, tn), functools.partial(lambda i, j, f: (0, f(j)), f=f))
               for f in w_col_blocks]
    return pl.pallas_call(
        kernel,
        grid=(t // tm, n_tiles),
        in_specs=[pl.BlockSpec((tm, d), lambda i, j: (i, 0))] + w_specs + extra_specs,
        out_specs=pl.BlockSpec((tm, tn), lambda i, j: (i, j)),
        out_shape=jax.ShapeDtypeStruct((t, n_tiles * tn), BF16),
        compiler_params=_params("parallel", "arbitrary"),
        name=name,
    )(xn, *([w] * len(w_col_blocks)), *extra_inputs)


def _rope_tables(seq, q_scale_unused=None):
    inv_freq = jnp.power(jnp.float32(ROPE_THETA),
                         -jnp.arange(ROPE_HALF, dtype=F32) * (2.0 / ROPE_DIM))
    ang = jnp.arange(seq).astype(F32)[:, None] * inv_freq[None, :]
    cos, sin = jnp.cos(ang), jnp.sin(ang)
    zeros = jnp.zeros((seq, HEAD_DIM - ROPE_DIM), F32)
    zh = jnp.zeros((seq, ROPE_HALF), F32)
    cos_t = jnp.concatenate([cos, cos, jnp.ones_like(zeros)], axis=-1)
    sa_t = jnp.concatenate([-sin, zh, zeros], axis=-1)
    sb_t = jnp.concatenate([zh, sin, zeros], axis=-1)
    return cos_t, sa_t, sb_t


def _attn_kernel(sink_ref, q_ref, kp_ref, kc_ref, kn_ref, vp_ref, vc_ref, vn_ref, g_ref, o_ref,
                 *, n_blocks):
    n = pl.program_id(1)
    blk = ATTN_BLOCK
    kk = jnp.concatenate([kp_ref[...], kc_ref[...], kn_ref[...]], axis=0)
    vv = jnp.concatenate([vp_ref[...], vc_ref[...], vn_ref[...]], axis=0)
    key = lax.broadcasted_iota(jnp.int32, (3 * blk, blk), 0)
    qry = lax.broadcasted_iota(jnp.int32, (3 * blk, blk), 1)
    rel = key - blk - qry
    valid = (jnp.abs(rel) <= WINDOW)
    valid = valid & ((key >= blk) | (n > 0)) & ((key < 2 * blk) | (n < n_blocks - 1))
    bias1 = jnp.where(valid, 0.0, MASK_VALUE).astype(F32)
    bias = jnp.concatenate([bias1] * GQA_GROUP, axis=1)
    lane_head = lax.broadcasted_iota(jnp.int32, (1, GQA_GROUP * blk), 1) // blk
    for h in range(N_KV_HEADS):
        qs = jnp.concatenate(
            [q_ref[:, (h * GQA_GROUP + g) * HEAD_DIM:(h * GQA_GROUP + g + 1) * HEAD_DIM]
             for g in range(GQA_GROUP)], axis=0)
        kh = kk[:, h * HEAD_DIM:(h + 1) * HEAD_DIM]
        vh = vv[:, h * HEAD_DIM:(h + 1) * HEAD_DIM]
        s = lax.dot_general(kh, qs, (((1,), (1,)), ((), ())), preferred_element_type=F32) + bias
        sink = jnp.zeros((1, GQA_GROUP * blk), F32)
        for g in range(GQA_GROUP):
            sink = jnp.where(lane_head == g, sink_ref[h * GQA_GROUP + g], sink)
        m = jnp.maximum(jnp.max(s, axis=0, keepdims=True), sink)
        p = jnp.exp(s - m)
        den = jnp.sum(p, axis=0, keepdims=True) + jnp.exp(sink - m)
        ot = lax.dot_general(vh, p.astype(BF16), (((0,), (0,)), ((), ())),
                             preferred_element_type=F32)
        ot = ot / den
        for g in range(GQA_GROUP):
            c0 = (h * GQA_GROUP + g) * HEAD_DIM
            gate = g_ref[:, c0:c0 + HEAD_DIM].astype(F32)
            o = ot[:, g * blk:(g + 1) * blk].T
            o_ref[:, c0:c0 + HEAD_DIM] = (o * gate).astype(o_ref.dtype)


def _attention(qkv, gates, sink, batch, seq):
    t = qkv.shape[0]
    blk = ATTN_BLOCK
    nb = seq // blk
    qw = N_Q_HEADS * HEAD_DIM
    kvw = N_KV_HEADS * HEAD_DIM
    k_col = qw // kvw
    v_col = k_col + 1

    def kv_spec(col, shift):
        def imap(b, n):
            return (b * nb + jnp.clip(n + shift, 0, nb - 1), col)
        return pl.BlockSpec((blk, kvw), imap)

    return pl.pallas_call(
        functools.partial(_attn_kernel, n_blocks=nb),
        grid=(batch, nb),
        in_specs=[pl.BlockSpec(memory_space=pltpu.SMEM),
                  pl.BlockSpec((blk, qw), lambda b, n: (b * nb + n, 0)),
                  kv_spec(k_col, -1), kv_spec(k_col, 0), kv_spec(k_col, 1),
                  kv_spec(v_col, -1), kv_spec(v_col, 0), kv_spec(v_col, 1),
                  pl.BlockSpec((blk, qw), lambda b, n: (b * nb + n, 0))],
        out_specs=pl.BlockSpec((blk, qw), lambda b, n: (b * nb + n, 0)),
        out_shape=jax.ShapeDtypeStruct((t, qw), BF16),
        compiler_params=_params("parallel", "arbitrary"),
        name="band_attention",
    )(sink, qkv, qkv, qkv, qkv, qkv, qkv, qkv, gates)


def _conv_kernel(hp_ref, hc_ref, hn_ref, dww_ref, dwb_ref, lng_ref, lnb_ref, pww_ref, pwb_ref,
                 g_ref, o_ref, hbuf, cbuf, *, n_tiles):
    i = pl.program_id(1)
    ts = hc_ref.shape[0]
    halo = BF16_SUBLANE_TILE
    width = hc_ref.shape[1]
    hbuf[0:halo, :] = jnp.where(i > 0, hp_ref[...].astype(F32), 0.0)
    hbuf[halo:halo + ts, :] = hc_ref[...].astype(F32)
    hbuf[halo + ts:, :] = jnp.where(i < n_tiles - 1, hn_ref[...].astype(F32), 0.0)
    base = halo - CONV_PAD
    rb = CONV_ROW_BLOCK
    span = rb + ((base + CONV_TAPS - 1) // SUBLANES) * SUBLANES
    for r0 in range(0, ts, rb):
        for c in range(width // LANES):
            cs = slice(c * LANES, (c + 1) * LANES)
            acc = jnp.broadcast_to(dwb_ref[:, cs], (rb, LANES))
            rows = hbuf[r0:r0 + span + SUBLANES, cs]
            for b in range(SUBLANES):
                taps = [t for t in range(CONV_TAPS) if (base + t) % SUBLANES == b]
                shifted = rows if b == 0 else pltpu.roll(rows, span + SUBLANES - b, 0)
                for t in taps:
                    a8 = base + t - b
                    acc = acc + shifted[a8:a8 + rb] * dww_ref[t:t + 1, cs]
            cbuf[r0:r0 + rb, cs] = acc
    y = cbuf[...]
    mu = jnp.mean(y, axis=-1, keepdims=True)
    yc = y - mu
    var = jnp.mean(yc * yc, axis=-1, keepdims=True)
    yn = yc * lax.rsqrt(var + LN_EPS) * lng_ref[...] + lnb_ref[...]
    z = (yn * jax.nn.sigmoid(yn)).astype(BF16)
    o = jnp.dot(z, pww_ref[...], preferred_element_type=F32) + pwb_ref[...]
    o_ref[...] = (o * g_ref[...].astype(F32)).astype(o_ref.dtype)


def _conformer_conv(hglu, gates, dw_w, dw_b, ln_g, ln_b, pw_w, pw_b, batch, seq, ts=256):
    t, width = hglu.shape
    nt = seq // ts
    halo = BF16_SUBLANE_TILE
    r = ts // halo
    nh = seq // halo
    row = lambda v: v.reshape(1, width)
    const = lambda b, i: (0, 0)
    return pl.pallas_call(
        functools.partial(_conv_kernel, n_tiles=nt),
        grid=(batch, nt),
        in_specs=[pl.BlockSpec((halo, width), lambda b, i: (b * nh + jnp.maximum(i * r - 1, 0), 0)),
                  pl.BlockSpec((ts, width), lambda b, i: (b * nt + i, 0)),
                  pl.BlockSpec((halo, width), lambda b, i: (b * nh + jnp.minimum((i + 1) * r, nh - 1), 0)),
                  pl.BlockSpec((CONV_TAPS, width), const),
                  pl.BlockSpec((1, width), const),
                  pl.BlockSpec((1, width), const),
                  pl.BlockSpec((1, width), const),
                  pl.BlockSpec((width, width), const),
                  pl.BlockSpec((1, width), const),
                  pl.BlockSpec((ts, width), lambda b, i: (b * nt + i, 1))],
        out_specs=pl.BlockSpec((ts, width), lambda b, i: (b * nt + i, 0)),
        out_shape=jax.ShapeDtypeStruct((t, width), BF16),
        scratch_shapes=[pltpu.VMEM((ts + 2 * halo, width), F32),
                        pltpu.VMEM((ts, width), F32)],
        compiler_params=_params("parallel", "arbitrary"),
        name="conformer_conv",
    )(hglu, hglu, hglu, dw_w, row(dw_b), row(ln_g), row(ln_b), pw_w, row(pw_b), gates)


def _out_kernel(a_ref, c_ref, wa_ref, wc_ref, x_ref, o_ref):
    acc = jnp.dot(a_ref[...], wa_ref[...], preferred_element_type=F32)
    acc = acc + jnp.dot(c_ref[...], wc_ref[...], preferred_element_type=F32)
    o_ref[...] = x_ref[...] + acc


def _out_proj(attn, conv, w, x2d, tm=1024, tn=512):
    t, d = x2d.shape
    ka, kc = attn.shape[1], conv.shape[1]
    assert ka == kc and w.shape[0] == ka + kc
    return pl.pallas_call(
        _out_kernel,
        grid=(t // tm, d // tn),
        in_specs=[pl.BlockSpec((tm, ka), lambda i, j: (i, 0)),
                  pl.BlockSpec((tm, kc), lambda i, j: (i, 0)),
                  pl.BlockSpec((ka, tn), lambda i, j: (0, j)),
                  pl.BlockSpec((kc, tn), lambda i, j: (1, j)),
                  pl.BlockSpec((tm, tn), lambda i, j: (i, j))],
        out_specs=pl.BlockSpec((tm, tn), lambda i, j: (i, j)),
        out_shape=jax.ShapeDtypeStruct((t, d), F32),
        compiler_params=_params("parallel", "arbitrary"),
        name="out_proj",
    )(attn, conv, w, w, x2d)


def _layer(x, norm_gain, w_in, q_gain, k_gain, sink, dw_w, dw_b, ln_g, ln_b, pw_w, pw_b, w_out):
    batch, seq, d = x.shape
    t = batch * seq
    qw = N_Q_HEADS * HEAD_DIM
    kvw = N_KV_HEADS * HEAD_DIM
    cw = dw_w.shape[1]
    x2d = x.reshape(t, d)

    c_v = qw + kvw
    c_ga = c_v + kvw
    c_ua = c_ga + qw
    c_ub = c_ua + cw
    c_gc = c_ub + cw
    w_bf = w_in.astype(BF16)

    xn = _rmsnorm(x2d, norm_gain)

    tm = 1024
    cos_t, sa_t, sb_t = _rope_tables(seq)
    scale = 1.0 / math.sqrt(HEAD_DIM)
    qg = (q_gain.astype(F32) * scale).reshape(1, HEAD_DIM)
    kg = k_gain.astype(F32).reshape(1, HEAD_DIM)
    tn_qkv = kvw
    rope_spec = pl.BlockSpec((tm, HEAD_DIM), lambda i, j: (i % (seq // tm), 0))
    gain_spec = pl.BlockSpec((1, HEAD_DIM), lambda i, j: (0, 0))
    qkv = _proj_call(functools.partial(_qkv_kernel, n_q_tiles=qw // tn_qkv), xn, w_bf,
                     [lambda j: j], [cos_t, sa_t, sb_t, qg, kg],
                     [rope_spec] * 3 + [gain_spec] * 2, tm, tn_qkv, c_ga // tn_qkv, "proj_qkv")
    tn_g = 1024
    n_ga = qw // tn_g
    gates = _proj_call(_silu_kernel, xn, w_bf,
                       [lambda j: jnp.where(j < n_ga, c_ga // tn_g + j, c_gc // tn_g + j - n_ga)],
                       [], [], tm, tn_g, (qw + cw) // tn_g, "proj_gates")
    tn_u = 512
    hglu = _proj_call(_glu_kernel, xn, w_bf,
                      [lambda j: c_ua // tn_u + j, lambda j: c_ub // tn_u + j],
                      [], [], tm, tn_u, cw // tn_u, "proj_glu")

    attn = _attention(qkv, gates, sink.astype(F32), batch, seq)
    conv = _conformer_conv(hglu, gates, dw_w, dw_b, ln_g, ln_b, pw_w.astype(BF16), pw_b, batch, seq)

    out = _out_proj(attn, conv, w_out.astype(BF16), x2d)
    return out.reshape(batch, seq, d)


def kernel(x, norm_gain, w_in, q_norm_gain, k_norm_gain, attn_sink, conv_dw_w, conv_dw_b,
           conv_ln_gain, conv_ln_bias, conv_pw_w, conv_pw_b, w_out):
    depth = norm_gain.shape[0]
    for layer in range(depth):
        x = _layer(x, norm_gain[layer], w_in[layer], q_norm_gain[layer], k_norm_gain[layer],
                   attn_sink[layer], conv_dw_w[layer], conv_dw_b[layer], conv_ln_gain[layer],
                   conv_ln_bias[layer], conv_pw_w[layer], conv_pw_b[layer], w_out[layer])
    return x
```

```python
import functools
import math

import jax
import jax.numpy as jnp
from jax import lax
from jax.experimental import pallas as pl
from jax.experimental.pallas import tpu as pltpu

F32 = jnp.float32
BF16 = jnp.bfloat16

HEAD_DIM = 128
N_Q_HEADS = 16
N_KV_HEADS = 4
GQA_GROUP = N_Q_HEADS // N_KV_HEADS
ATTN_BLOCK = 128
WINDOW = 128
ROPE_THETA = 500000.0
ROPE_DIM = HEAD_DIM // 4
ROPE_HALF = ROPE_DIM // 2
CONV_TAPS = 31
CONV_PAD = CONV_TAPS // 2
NORM_EPS = 1e-6
LN_EPS = 1e-5
MASK_VALUE = -1e30
LOG2E = math.log2(math.e)
ATTN_BLOCKS_PER_STEP = 2
EPILOGUE_ROWS = 256

LANES = 128
SUBLANES = 8
CONV_ROW_BLOCK = 128
BF16_SUBLANE_TILE = 16
VMEM_LIMIT_BYTES = 56 * 1024 * 1024


def _params(*sem):
    return pltpu.CompilerParams(dimension_semantics=sem, vmem_limit_bytes=VMEM_LIMIT_BYTES)


def _rmsnorm_kernel(x_ref, g_ref, o_ref):
    x = x_ref[...]
    ms = jnp.mean(x * x, axis=-1, keepdims=True)
    o_ref[...] = (x * lax.rsqrt(ms + NORM_EPS) * g_ref[...]).astype(o_ref.dtype)


def _rmsnorm(x2d, gain, tm=256):
    t, d = x2d.shape
    return pl.pallas_call(
        _rmsnorm_kernel,
        grid=(t // tm,),
        in_specs=[pl.BlockSpec((tm, d), lambda i: (i, 0)),
                  pl.BlockSpec((1, d), lambda i: (0, 0))],
        out_specs=pl.BlockSpec((tm, d), lambda i: (i, 0)),
        out_shape=jax.ShapeDtypeStruct((t, d), BF16),
        compiler_params=_params("parallel"),
        name="rmsnorm",
    )(x2d, gain.reshape(1, d))


def _qk_kernel(xn_ref, w_ref, cos_ref, sa_ref, sb_ref, gain_ref, o_ref, *, n_norm_heads):
    gain = gain_ref[...]
    rest = n_norm_heads * HEAD_DIM
    for r0 in range(0, xn_ref.shape[0], EPILOGUE_ROWS):
        rows = slice(r0, r0 + EPILOGUE_ROWS)
        acc = jnp.dot(xn_ref[rows, :], w_ref[...], preferred_element_type=F32)
        cos, sa, sb = cos_ref[rows, :], sa_ref[rows, :], sb_ref[rows, :]
        for h in range(n_norm_heads):
            xh = acc[:, h * HEAD_DIM:(h + 1) * HEAD_DIM]
            ms = jnp.mean(xh * xh, axis=-1, keepdims=True)
            y = xh * lax.rsqrt(ms + NORM_EPS) * gain
            y = (y * cos + pltpu.roll(y, HEAD_DIM - ROPE_HALF, 1) * sa
                 + pltpu.roll(y, ROPE_HALF, 1) * sb)
            o_ref[rows, h * HEAD_DIM:(h + 1) * HEAD_DIM] = y.astype(o_ref.dtype)
        if rest < acc.shape[1]:
            o_ref[rows, rest:] = acc[:, rest:].astype(o_ref.dtype)


def _silu_kernel(xn_ref, w_ref, o_ref):
    acc = jnp.dot(xn_ref[...], w_ref[...], preferred_element_type=F32)
    o_ref[...] = (acc * jax.nn.sigmoid(acc)).astype(o_ref.dtype)


def _glu_kernel(xn_ref, wa_ref, wb_ref, o_ref):
    xn = xn_ref[...]
    a = jnp.dot(xn, wa_ref[...], preferred_element_type=F32)
    b = jnp.dot(xn, wb_ref[...], preferred_element_type=F32)
    o_ref[...] = (a * jax.nn.sigmoid(b)).astype(o_ref.dtype)


def _proj_call(kernel, xn, w, w_col_blocks, extra_inputs, extra_specs, tm, tn, n_tiles, name):
    t, d = xn.shape
    w_specs = [pl.BlockSpec((d, tn), functools.partial(lambda i, j, f: (0, f(j)), f=f))
               for f in w_col_blocks]
    return pl.pallas_call(
        kernel,
        grid=(t // tm, n_tiles),
        in_specs=[pl.BlockSpec((tm, d), lambda i, j: (i, 0))] + w_specs + extra_specs,
        out_specs=pl.BlockSpec((tm, tn), lambda i, j: (i, j)),
        out_shape=jax.ShapeDtypeStruct((t, n_tiles * tn), BF16),
        compiler_params=_params("parallel", "arbitrary"),
        name=name,
    )(xn, *([w] * len(w_col_blocks)), *extra_inputs)


def _rope_tables(seq, q_scale_unused=None):
    inv_freq = jnp.power(jnp.float32(ROPE_THETA),
                         -jnp.arange(ROPE_HALF, dtype=F32) * (2.0 / ROPE_DIM))
    ang = jnp.arange(seq).astype(F32)[:, None] * inv_freq[None, :]
    cos, sin = jnp.cos(ang), jnp.sin(ang)
    zeros = jnp.zeros((seq, HEAD_DIM - ROPE_DIM), F32)
    zh = jnp.zeros((seq, ROPE_HALF), F32)
    cos_t = jnp.concatenate([cos, cos, jnp.ones_like(zeros)], axis=-1)
    sa_t = jnp.concatenate([-sin, zh, zeros], axis=-1)
    sb_t = jnp.concatenate([zh, sin, zeros], axis=-1)
    return cos_t, sa_t, sb_t


def _attn_kernel(sink_ref, q_ref, kvp_ref, kvc_ref, kvn_ref, g_ref, o_ref, *, n_steps):
    step = pl.program_id(1)
    blk = ATTN_BLOCK
    kvw = N_KV_HEADS * HEAD_DIM
    kv = jnp.concatenate([kvp_ref[...], kvc_ref[...], kvn_ref[...]], axis=0)
    key = lax.broadcasted_iota(jnp.int32, (3 * blk, blk), 0)
    qry = lax.broadcasted_iota(jnp.int32, (3 * blk, blk), 1)
    in_band = jnp.abs(key - blk - qry) <= WINDOW
    lane_head = lax.broadcasted_iota(jnp.int32, (1, GQA_GROUP * blk), 1) // blk
    for u in range(ATTN_BLOCKS_PER_STEP):
        valid = in_band
        if u == 0:
            valid = valid & ((key >= blk) | (step > 0))
        if u == ATTN_BLOCKS_PER_STEP - 1:
            valid = valid & ((key < 2 * blk) | (step < n_steps - 1))
        bias1 = jnp.where(valid, 0.0, MASK_VALUE).astype(F32)
        bias = jnp.concatenate([bias1] * GQA_GROUP, axis=1)
        rows = slice(u * blk, (u + 1) * blk)
        for h in range(N_KV_HEADS):
            qs = jnp.concatenate(
                [q_ref[rows, (h * GQA_GROUP + g) * HEAD_DIM:(h * GQA_GROUP + g + 1) * HEAD_DIM]
                 for g in range(GQA_GROUP)], axis=0)
            kh = kv[u * blk:(u + 3) * blk, h * HEAD_DIM:(h + 1) * HEAD_DIM]
            vh = kv[u * blk:(u + 3) * blk, kvw + h * HEAD_DIM:kvw + (h + 1) * HEAD_DIM]
            s = lax.dot_general(kh, qs, (((1,), (1,)), ((), ())),
                                preferred_element_type=F32) + bias
            sink = jnp.zeros((1, GQA_GROUP * blk), F32)
            for g in range(GQA_GROUP):
                sink = jnp.where(lane_head == g, sink_ref[h * GQA_GROUP + g] * LOG2E, sink)
            m = jnp.maximum(jnp.max(s, axis=0, keepdims=True), sink)
            p = jnp.exp2(s - m)
            den = jnp.sum(p, axis=0, keepdims=True) + jnp.exp2(sink - m)
            ot = lax.dot_general(vh, p.astype(BF16), (((0,), (0,)), ((), ())),
                                 preferred_element_type=F32)
            ot = ot / den
            for g in range(GQA_GROUP):
                c0 = (h * GQA_GROUP + g) * HEAD_DIM
                gate = g_ref[rows, c0:c0 + HEAD_DIM].astype(F32)
                o = ot[:, g * blk:(g + 1) * blk].T
                o_ref[rows, c0:c0 + HEAD_DIM] = (o * gate).astype(o_ref.dtype)


def _attention(q, kv, gates, sink, batch, seq):
    t, qw = q.shape
    blk = ATTN_BLOCK
    u = ATTN_BLOCKS_PER_STEP
    nb = seq // blk
    ns = nb // u
    kv2 = kv.shape[1]
    return pl.pallas_call(
        functools.partial(_attn_kernel, n_steps=ns),
        grid=(batch, ns),
        in_specs=[pl.BlockSpec(memory_space=pltpu.SMEM),
                  pl.BlockSpec((u * blk, qw), lambda b, s: (b * ns + s, 0)),
                  pl.BlockSpec((blk, kv2), lambda b, s: (b * nb + jnp.maximum(s * u - 1, 0), 0)),
                  pl.BlockSpec((u * blk, kv2), lambda b, s: (b * ns + s, 0)),
                  pl.BlockSpec((blk, kv2), lambda b, s: (b * nb + jnp.minimum(s * u + u, nb - 1), 0)),
                  pl.BlockSpec((u * blk, qw), lambda b, s: (b * ns + s, 0))],
        out_specs=pl.BlockSpec((u * blk, qw), lambda b, s: (b * ns + s, 0)),
        out_shape=jax.ShapeDtypeStruct((t, qw), BF16),
        compiler_params=_params("parallel", "arbitrary"),
        name="band_attention",
    )(sink, q, kv, kv, kv, gates)


def _conv_kernel(hp_ref, hc_ref, hn_ref, dww_ref, dwb_ref, lng_ref, lnb_ref, pww_ref, pwb_ref,
                 g_ref, o_ref, hbuf, cbuf, *, n_tiles):
    i = pl.program_id(1)
    ts = hc_ref.shape[0]
    halo = BF16_SUBLANE_TILE
    width = hc_ref.shape[1]
    hbuf[0:halo, :] = jnp.where(i > 0, hp_ref[...].astype(F32), 0.0)
    hbuf[halo:halo + ts, :] = hc_ref[...].astype(F32)
    hbuf[halo + ts:, :] = jnp.where(i < n_tiles - 1, hn_ref[...].astype(F32), 0.0)
    base = halo - CONV_PAD
    rb = CONV_ROW_BLOCK
    span = rb + ((base + CONV_TAPS - 1) // SUBLANES) * SUBLANES
    for r0 in range(0, ts, rb):
        for c in range(width // LANES):
            cs = slice(c * LANES, (c + 1) * LANES)
            acc = jnp.broadcast_to(dwb_ref[:, cs], (rb, LANES))
            rows = hbuf[r0:r0 + span + SUBLANES, cs]
            for b in range(SUBLANES):
                taps = [t for t in range(CONV_TAPS) if (base + t) % SUBLANES == b]
                shifted = rows if b == 0 else pltpu.roll(rows, span + SUBLANES - b, 0)
                for t in taps:
                    a8 = base + t - b
                    acc = acc + shifted[a8:a8 + rb] * dww_ref[t:t + 1, cs]
            cbuf[r0:r0 + rb, cs] = acc
    y = cbuf[...]
    mu = jnp.mean(y, axis=-1, keepdims=True)
    yc = y - mu
    var = jnp.mean(yc * yc, axis=-1, keepdims=True)
    yn = yc * lax.rsqrt(var + LN_EPS) * lng_ref[...] + lnb_ref[...]
    z = (yn * jax.nn.sigmoid(yn)).astype(BF16)
    o = jnp.dot(z, pww_ref[...], preferred_element_type=F32) + pwb_ref[...]
    o_ref[...] = (o * g_ref[...].astype(F32)).astype(o_ref.dtype)


def _conformer_conv(hglu, gates, dw_w, dw_b, ln_g, ln_b, pw_w, pw_b, batch, seq, ts=256):
    t, width = hglu.shape
    nt = seq // ts
    halo = BF16_SUBLANE_TILE
    r = ts // halo
    nh = seq // halo
    row = lambda v: v.reshape(1, width)
    const = lambda b, i: (0, 0)
    return pl.pallas_call(
        functools.partial(_conv_kernel, n_tiles=nt),
        grid=(batch, nt),
        in_specs=[pl.BlockSpec((halo, width), lambda b, i: (b * nh + jnp.maximum(i * r - 1, 0), 0)),
                  pl.BlockSpec((ts, width), lambda b, i: (b * nt + i, 0)),
                  pl.BlockSpec((halo, width), lambda b, i: (b * nh + jnp.minimum((i + 1) * r, nh - 1), 0)),
                  pl.BlockSpec((CONV_TAPS, width), const),
                  pl.BlockSpec((1, width), const),
                  pl.BlockSpec((1, width), const),
                  pl.BlockSpec((1, width), const),
                  pl.BlockSpec((width, width), const),
                  pl.BlockSpec((1, width), const),
                  pl.BlockSpec((ts, width), lambda b, i: (b * nt + i, 1))],
        out_specs=pl.BlockSpec((ts, width), lambda b, i: (b * nt + i, 0)),
        out_shape=jax.ShapeDtypeStruct((t, width), BF16),
        scratch_shapes=[pltpu.VMEM((ts + 2 * halo, width), F32),
                        pltpu.VMEM((ts, width), F32)],
        compiler_params=_params("parallel", "arbitrary"),
        name="conformer_conv",
    )(hglu, hglu, hglu, dw_w, row(dw_b), row(ln_g), row(ln_b), pw_w, row(pw_b), gates)


def _out_kernel(a_ref, c_ref, wa_ref, wc_ref, x_ref, o_ref):
    acc = jnp.dot(a_ref[...], wa_ref[...], preferred_element_type=F32)
    acc = acc + jnp.dot(c_ref[...], wc_ref[...], preferred_element_type=F32)
    o_ref[...] = x_ref[...] + acc


def _out_proj(attn, conv, w, x2d, tm=1024, tn=512):
    t, d = x2d.shape
    ka, kc = attn.shape[1], conv.shape[1]
    assert ka == kc and w.shape[0] == ka + kc
    return pl.pallas_call(
        _out_kernel,
        grid=(t // tm, d // tn),
        in_specs=[pl.BlockSpec((tm, ka), lambda i, j: (i, 0)),
                  pl.BlockSpec((tm, kc), lambda i, j: (i, 0)),
                  pl.BlockSpec((ka, tn), lambda i, j: (0, j)),
                  pl.BlockSpec((kc, tn), lambda i, j: (1, j)),
                  pl.BlockSpec((tm, tn), lambda i, j: (i, j))],
        out_specs=pl.BlockSpec((tm, tn), lambda i, j: (i, j)),
        out_shape=jax.ShapeDtypeStruct((t, d), F32),
        compiler_params=_params("parallel", "arbitrary"),
        name="out_proj",
    )(attn, conv, w, w, x2d)


def _layer(x, norm_gain, w_in, q_gain, k_gain, sink, dw_w, dw_b, ln_g, ln_b, pw_w, pw_b, w_out):
    batch, seq, d = x.shape
    t = batch * seq
    qw = N_Q_HEADS * HEAD_DIM
    kvw = N_KV_HEADS * HEAD_DIM
    cw = dw_w.shape[1]
    x2d = x.reshape(t, d)

    c_v = qw + kvw
    c_ga = c_v + kvw
    c_ua = c_ga + qw
    c_ub = c_ua + cw
    c_gc = c_ub + cw
    w_bf = w_in.astype(BF16)

    xn = _rmsnorm(x2d, norm_gain)

    tm = 1024
    cos_t, sa_t, sb_t = _rope_tables(seq)
    qg = (q_gain.astype(F32) * (LOG2E / math.sqrt(HEAD_DIM))).reshape(1, HEAD_DIM)
    kg = k_gain.astype(F32).reshape(1, HEAD_DIM)
    rope_spec = pl.BlockSpec((tm, HEAD_DIM), lambda i, j: (i % (seq // tm), 0))
    gain_spec = pl.BlockSpec((1, HEAD_DIM), lambda i, j: (0, 0))
    tn_q = 1024
    q = _proj_call(functools.partial(_qk_kernel, n_norm_heads=tn_q // HEAD_DIM), xn, w_bf,
                   [lambda j: j], [cos_t, sa_t, sb_t, qg],
                   [rope_spec] * 3 + [gain_spec], tm, tn_q, qw // tn_q, "proj_q")
    tn_kv = 2 * kvw
    kv = _proj_call(functools.partial(_qk_kernel, n_norm_heads=N_KV_HEADS), xn, w_bf,
                    [lambda j: qw // tn_kv + j], [cos_t, sa_t, sb_t, kg],
                    [rope_spec] * 3 + [gain_spec], tm, tn_kv, 1, "proj_kv")
    tn_g = 1024
    n_ga = qw // tn_g
    gates = _proj_call(_silu_kernel, xn, w_bf,
                       [lambda j: jnp.where(j < n_ga, c_ga // tn_g + j, c_gc // tn_g + j - n_ga)],
                       [], [], tm, tn_g, (qw + cw) // tn_g, "proj_gates")
    tn_u = 512
    hglu = _proj_call(_glu_kernel, xn, w_bf,
                      [lambda j: c_ua // tn_u + j, lambda j: c_ub // tn_u + j],
                      [], [], tm, tn_u, cw // tn_u, "proj_glu")

    attn = _attention(q, kv, gates, sink.astype(F32), batch, seq)
    conv = _conformer_conv(hglu, gates, dw_w, dw_b, ln_g, ln_b, pw_w.astype(BF16), pw_b, batch, seq)

    out = _out_proj(attn, conv, w_out.astype(BF16), x2d)
    return out.reshape(batch, seq, d)


def kernel(x, norm_gain, w_in, q_norm_gain, k_norm_gain, attn_sink, conv_dw_w, conv_dw_b,
           conv_ln_gain, conv_ln_bias, conv_pw_w, conv_pw_b, w_out):
    depth = norm_gain.shape[0]
    for layer in range(depth):
        x = _layer(x, norm_gain[layer], w_in[layer], q_norm_gain[layer], k_norm_gain[layer],
                   attn_sink[layer], conv_dw_w[layer], conv_dw_b[layer], conv_ln_gain[layer],
                   conv_ln_bias[layer], conv_pw_w[layer], conv_pw_b[layer], w_out[layer])
    return x
```

```python
import functools
import math

import jax
import jax.numpy as jnp
from jax import lax
from jax.experimental import pallas as pl
from jax.experimental.pallas import tpu as pltpu

F32 = jnp.float32
BF16 = jnp.bfloat16

HEAD_DIM = 128
N_Q_HEADS = 16
N_KV_HEADS = 4
GQA_GROUP = N_Q_HEADS // N_KV_HEADS
ATTN_BLOCK = 128
WINDOW = 128
ROPE_THETA = 500000.0
ROPE_DIM = HEAD_DIM // 4
ROPE_HALF = ROPE_DIM // 2
CONV_TAPS = 31
CONV_PAD = CONV_TAPS // 2
NORM_EPS = 1e-6
LN_EPS = 1e-5
MASK_VALUE = -1e30
LOG2E = math.log2(math.e)
ATTN_BLOCKS_PER_STEP = 4
EPILOGUE_ROWS = 256
NORM_ROWS = 64

LANES = 128
SUBLANES = 8
MXU_COLS = 256
CONV_ROW_BLOCK = 64
CONV_SHIFT_WINDOW = 128
POINTWISE_ROWS = 256
BF16_SUBLANE_TILE = 16
VMEM_LIMIT_BYTES = 56 * 1024 * 1024


def _params(*sem):
    return pltpu.CompilerParams(dimension_semantics=sem, vmem_limit_bytes=VMEM_LIMIT_BYTES)


def _qk_tile(xn_ref, w_ref, cos_ref, sa_ref, sb_ref, gain_ref, o_ref, n_norm_heads):
    gain = gain_ref[...]
    rest = n_norm_heads * HEAD_DIM
    for r0 in range(0, xn_ref.shape[0], EPILOGUE_ROWS):
        rows = slice(r0, r0 + EPILOGUE_ROWS)
        acc = jnp.dot(xn_ref[rows, :], w_ref[...], preferred_element_type=F32)
        cos, sa, sb = cos_ref[rows, :], sa_ref[rows, :], sb_ref[rows, :]
        for h in range(n_norm_heads):
            xh = acc[:, h * HEAD_DIM:(h + 1) * HEAD_DIM]
            ms = jnp.mean(xh * xh, axis=-1, keepdims=True)
            y = xh * lax.rsqrt(ms + NORM_EPS) * gain
            y = (y * cos + pltpu.roll(y, HEAD_DIM - ROPE_HALF, 1) * sa
                 + pltpu.roll(y, ROPE_HALF, 1) * sb)
            o_ref[rows, h * HEAD_DIM:(h + 1) * HEAD_DIM] = y.astype(o_ref.dtype)
        if rest < acc.shape[1]:
            o_ref[rows, rest:] = acc[:, rest:].astype(o_ref.dtype)


def _q_kernel(xn_ref, w_ref, cos_ref, sa_ref, sb_ref, gain_ref, o_ref):
    _qk_tile(xn_ref, w_ref, cos_ref, sa_ref, sb_ref, gain_ref, o_ref, o_ref.shape[1] // HEAD_DIM)


def _norm_kv_kernel(x_ref, ng_ref, w_ref, cos_ref, sa_ref, sb_ref, gain_ref, xn_ref, kv_ref):
    ng = ng_ref[...]
    for r0 in range(0, x_ref.shape[0], NORM_ROWS):
        rows = slice(r0, r0 + NORM_ROWS)
        x = x_ref[rows, :]
        ms = jnp.mean(x * x, axis=-1, keepdims=True)
        xn_ref[rows, :] = (x * lax.rsqrt(ms + NORM_EPS) * ng).astype(xn_ref.dtype)
    _qk_tile(xn_ref, w_ref, cos_ref, sa_ref, sb_ref, gain_ref, kv_ref, N_KV_HEADS)


def _norm_kv_proj(x2d, norm_gain, w, w_col_block, cos_t, sa_t, sb_t, k_gain, seq, tm=512):
    t, d = x2d.shape
    tn = 2 * N_KV_HEADS * HEAD_DIM
    rope_spec = pl.BlockSpec((tm, HEAD_DIM), lambda i: (i % (seq // tm), 0))
    once = dict(pipeline_mode=pl.Buffered(1))
    return pl.pallas_call(
        _norm_kv_kernel,
        grid=(t // tm,),
        in_specs=[pl.BlockSpec((tm, d), lambda i: (i, 0)),
                  pl.BlockSpec((1, d), lambda i: (0, 0), **once),
                  pl.BlockSpec((d, tn), lambda i: (0, w_col_block), **once),
                  rope_spec, rope_spec, rope_spec,
                  pl.BlockSpec((1, HEAD_DIM), lambda i: (0, 0), **once)],
        out_specs=[pl.BlockSpec((tm, d), lambda i: (i, 0)),
                   pl.BlockSpec((tm, tn), lambda i: (i, 0))],
        out_shape=[jax.ShapeDtypeStruct((t, d), BF16),
                   jax.ShapeDtypeStruct((t, tn), BF16)],
        compiler_params=_params("parallel"),
        name="norm_proj_kv",
    )(x2d, norm_gain.reshape(1, d), w, cos_t, sa_t, sb_t, k_gain)


def _silu_kernel(xn_ref, w_ref, o_ref):
    acc = jnp.dot(xn_ref[...], w_ref[...], preferred_element_type=F32)
    o_ref[...] = (acc * jax.nn.sigmoid(acc)).astype(o_ref.dtype)


def _glu_kernel(xn_ref, wa_ref, wb_ref, o_ref):
    xn = xn_ref[...]
    a = jnp.dot(xn, wa_ref[...], preferred_element_type=F32)
    b = jnp.dot(xn, wb_ref[...], preferred_element_type=F32)
    o_ref[...] = (a * jax.nn.sigmoid(b)).astype(o_ref.dtype)


def _proj_call(kernel, xn, w, w_col_blocks, extra_inputs, extra_specs, tm, tn, n_tiles, name):
    t, d = xn.shape
    w_specs = [pl.BlockSpec((d, tn), functools.partial(lambda i, j, f: (0, f(j)), f=f))
               for f in w_col_blocks]
    return pl.pallas_call(
        kernel,
        grid=(t // tm, n_tiles),
        in_specs=[pl.BlockSpec((tm, d), lambda i, j: (i, 0))] + w_specs + extra_specs,
        out_specs=pl.BlockSpec((tm, tn), lambda i, j: (i, j)),
        out_shape=jax.ShapeDtypeStruct((t, n_tiles * tn), BF16),
        compiler_params=_params("parallel", "arbitrary"),
        name=name,
    )(xn, *([w] * len(w_col_blocks)), *extra_inputs)


def _rope_tables(seq):
    inv_freq = jnp.power(jnp.float32(ROPE_THETA),
                         -jnp.arange(ROPE_HALF, dtype=F32) * (2.0 / ROPE_DIM))
    ang = jnp.arange(seq).astype(F32)[:, None] * inv_freq[None, :]
    cos, sin = jnp.cos(ang), jnp.sin(ang)
    zeros = jnp.zeros((seq, HEAD_DIM - ROPE_DIM), F32)
    zh = jnp.zeros((seq, ROPE_HALF), F32)
    cos_t = jnp.concatenate([cos, cos, jnp.ones_like(zeros)], axis=-1)
    sa_t = jnp.concatenate([-sin, zh, zeros], axis=-1)
    sb_t = jnp.concatenate([zh, sin, zeros], axis=-1)
    return cos_t, sa_t, sb_t


def _attn_kernel(sink_ref, q_ref, kvp_ref, kvc_ref, kvn_ref, g_ref, o_ref, *, n_steps):
    step = pl.program_id(1)
    blk = ATTN_BLOCK
    kvw = N_KV_HEADS * HEAD_DIM
    kv = jnp.concatenate([kvp_ref[...], kvc_ref[...], kvn_ref[...]], axis=0)
    key = lax.broadcasted_iota(jnp.int32, (3 * blk, blk), 0)
    qry = lax.broadcasted_iota(jnp.int32, (3 * blk, blk), 1)
    in_band = jnp.abs(key - blk - qry) <= WINDOW
    lane_head = lax.broadcasted_iota(jnp.int32, (1, GQA_GROUP * blk), 1) // blk
    for u in range(ATTN_BLOCKS_PER_STEP):
        valid = in_band
        if u == 0:
            valid = valid & ((key >= blk) | (step > 0))
        if u == ATTN_BLOCKS_PER_STEP - 1:
            valid = valid & ((key < 2 * blk) | (step < n_steps - 1))
        bias1 = jnp.where(valid, 0.0, MASK_VALUE).astype(F32)
        bias = jnp.concatenate([bias1] * GQA_GROUP, axis=1)
        rows = slice(u * blk, (u + 1) * blk)
        for h in range(N_KV_HEADS):
            qs = jnp.concatenate(
                [q_ref[rows, (h * GQA_GROUP + g) * HEAD_DIM:(h * GQA_GROUP + g + 1) * HEAD_DIM]
                 for g in range(GQA_GROUP)], axis=0)
            kh = kv[u * blk:(u + 3) * blk, h * HEAD_DIM:(h + 1) * HEAD_DIM]
            vh = kv[u * blk:(u + 3) * blk, kvw + h * HEAD_DIM:kvw + (h + 1) * HEAD_DIM]
            s = lax.dot_general(kh, qs, (((1,), (1,)), ((), ())),
                                preferred_element_type=F32) + bias
            sink = jnp.zeros((1, GQA_GROUP * blk), F32)
            for g in range(GQA_GROUP):
                sink = jnp.where(lane_head == g, sink_ref[h * GQA_GROUP + g] * LOG2E, sink)
            m = jnp.maximum(jnp.max(s, axis=0, keepdims=True), sink)
            p = jnp.exp2(s - m)
            den = jnp.sum(p, axis=0, keepdims=True) + jnp.exp2(sink - m)
            ot = lax.dot_general(vh, p.astype(BF16), (((0,), (0,)), ((), ())),
                                 preferred_element_type=F32)
            ot = ot / den
            for g in range(GQA_GROUP):
                c0 = (h * GQA_GROUP + g) * HEAD_DIM
                gate = g_ref[rows, c0:c0 + HEAD_DIM].astype(F32)
                o = ot[:, g * blk:(g + 1) * blk].T
                o_ref[rows, c0:c0 + HEAD_DIM] = (o * gate).astype(o_ref.dtype)


def _attention(q, kv, gates, sink, batch, seq):
    t, qw = q.shape
    blk = ATTN_BLOCK
    u = ATTN_BLOCKS_PER_STEP
    nb = seq // blk
    ns = nb // u
    kv2 = kv.shape[1]
    return pl.pallas_call(
        functools.partial(_attn_kernel, n_steps=ns),
        grid=(batch, ns),
        in_specs=[pl.BlockSpec(memory_space=pltpu.SMEM),
                  pl.BlockSpec((u * blk, qw), lambda b, s: (b * ns + s, 0)),
                  pl.BlockSpec((blk, kv2), lambda b, s: (b * nb + jnp.maximum(s * u - 1, 0), 0)),
                  pl.BlockSpec((u * blk, kv2), lambda b, s: (b * ns + s, 0)),
                  pl.BlockSpec((blk, kv2), lambda b, s: (b * nb + jnp.minimum(s * u + u, nb - 1), 0)),
                  pl.BlockSpec((u * blk, qw), lambda b, s: (b * ns + s, 0))],
        out_specs=pl.BlockSpec((u * blk, qw), lambda b, s: (b * ns + s, 0)),
        out_shape=jax.ShapeDtypeStruct((t, qw), BF16),
        compiler_params=_params("parallel", "arbitrary"),
        name="band_attention",
    )(sink, q, kv, kv, kv, gates)


def _conv_shift_geometry():
    base = BF16_SUBLANE_TILE - CONV_PAD
    span = CONV_ROW_BLOCK + ((base + CONV_TAPS - 1) // SUBLANES) * SUBLANES
    return base, span


def _conv_shift_matrix():
    _, span = _conv_shift_geometry()
    i = jnp.arange(SUBLANES * span)
    sel = (i % span + i // span)[:, None] == jnp.arange(CONV_SHIFT_WINDOW)[None, :]
    return sel.astype(BF16)


def _conv_kernel(hp_ref, hc_ref, hn_ref, shift_ref, dww_ref, dwb_ref, lng_ref, lnb_ref, pww_ref,
                 pwb_ref, g_ref, o_ref, hbuf, cbuf, *, n_tiles):
    i = pl.program_id(1)
    ts = hc_ref.shape[0]
    halo = BF16_SUBLANE_TILE
    width = hc_ref.shape[1]
    hbuf[0:halo, :] = jnp.where(i > 0, hp_ref[...], jnp.zeros_like(hp_ref[...]))
    hbuf[halo:halo + ts, :] = hc_ref[...]
    hbuf[halo + ts:2 * halo + ts, :] = jnp.where(i < n_tiles - 1, hn_ref[...],
                                                 jnp.zeros_like(hn_ref[...]))
    hbuf[2 * halo + ts:, :] = jnp.zeros((hbuf.shape[0] - 2 * halo - ts, width), hbuf.dtype)
    base, span = _conv_shift_geometry()
    rb = CONV_ROW_BLOCK
    for r0 in range(0, ts, rb):
        for n0 in range(0, width, MXU_COLS):
            window = hbuf[r0:r0 + CONV_SHIFT_WINDOW, n0:n0 + MXU_COLS]
            lanes = [slice(n0 + c0, n0 + c0 + LANES) for c0 in range(0, MXU_COLS, LANES)]
            accs = [jnp.broadcast_to(dwb_ref[:, cs], (rb, LANES)) for cs in lanes]
            for b in range(SUBLANES):
                shifted = jnp.dot(shift_ref[b * span:(b + 1) * span, :], window,
                                  preferred_element_type=F32)
                for t in range(CONV_TAPS):
                    if (base + t) % SUBLANES == b:
                        a8 = base + t - b
                        for k, cs in enumerate(lanes):
                            accs[k] = accs[k] + (shifted[a8:a8 + rb, k * LANES:(k + 1) * LANES]
                                                 * dww_ref[t:t + 1, cs])
            for k, cs in enumerate(lanes):
                cbuf[r0:r0 + rb, cs] = accs[k]
        if (r0 + rb) % POINTWISE_ROWS == 0:
            rows = slice(r0 + rb - POINTWISE_ROWS, r0 + rb)
            y = cbuf[rows, :]
            mu = jnp.mean(y, axis=-1, keepdims=True)
            yc = y - mu
            var = jnp.mean(yc * yc, axis=-1, keepdims=True)
            yn = yc * lax.rsqrt(var + LN_EPS) * lng_ref[...] + lnb_ref[...]
            z = (yn * jax.nn.sigmoid(yn)).astype(BF16)
            o = jnp.dot(z, pww_ref[...], preferred_element_type=F32) + pwb_ref[...]
            o_ref[rows, :] = (o * g_ref[rows, :].astype(F32)).astype(o_ref.dtype)


def _conformer_conv(hglu, gates, dw_w, dw_b, ln_g, ln_b, pw_w, pw_b, batch, seq, ts=256):
    t, width = hglu.shape
    nt = seq // ts
    halo = BF16_SUBLANE_TILE
    r = ts // halo
    nh = seq // halo
    row = lambda v: v.reshape(1, width)
    const = lambda b, i: (0, 0)
    once = dict(pipeline_mode=pl.Buffered(1))
    shift = _conv_shift_matrix()
    return pl.pallas_call(
        functools.partial(_conv_kernel, n_tiles=nt),
        grid=(batch, nt),
        in_specs=[pl.BlockSpec((halo, width), lambda b, i: (b * nh + jnp.maximum(i * r - 1, 0), 0)),
                  pl.BlockSpec((ts, width), lambda b, i: (b * nt + i, 0)),
                  pl.BlockSpec((halo, width), lambda b, i: (b * nh + jnp.minimum((i + 1) * r, nh - 1), 0)),
                  pl.BlockSpec(shift.shape, const, **once),
                  pl.BlockSpec((CONV_TAPS, width), const, **once),
                  pl.BlockSpec((1, width), const, **once),
                  pl.BlockSpec((1, width), const, **once),
                  pl.BlockSpec((1, width), const, **once),
                  pl.BlockSpec((width, width), const, **once),
                  pl.BlockSpec((1, width), const, **once),
                  pl.BlockSpec((ts, width), lambda b, i: (b * nt + i, 1))],
        out_specs=pl.BlockSpec((ts, width), lambda b, i: (b * nt + i, 0)),
        out_shape=jax.ShapeDtypeStruct((t, width), BF16),
        scratch_shapes=[pltpu.VMEM((ts - CONV_ROW_BLOCK + CONV_SHIFT_WINDOW, width), BF16),
                        pltpu.VMEM((ts, width), F32)],
        compiler_params=_params("parallel", "arbitrary"),
        name="conformer_conv",
    )(hglu, hglu, hglu, shift, dw_w, row(dw_b), row(ln_g), row(ln_b), pw_w, row(pw_b), gates)


def _out_kernel(a_ref, c_ref, wa_ref, wc_ref, x_ref, o_ref):
    acc = jnp.dot(a_ref[...], wa_ref[...], preferred_element_type=F32)
    acc = acc + jnp.dot(c_ref[...], wc_ref[...], preferred_element_type=F32)
    o_ref[...] = x_ref[...] + acc


def _out_proj(attn, conv, w, x2d, tm=1024, tn=512):
    t, d = x2d.shape
    ka, kc = attn.shape[1], conv.shape[1]
    assert ka == kc and w.shape[0] == ka + kc
    return pl.pallas_call(
        _out_kernel,
        grid=(t // tm, d // tn),
        in_specs=[pl.BlockSpec((tm, ka), lambda i, j: (i, 0)),
                  pl.BlockSpec((tm, kc), lambda i, j: (i, 0)),
                  pl.BlockSpec((ka, tn), lambda i, j: (0, j)),
                  pl.BlockSpec((kc, tn), lambda i, j: (1, j)),
                  pl.BlockSpec((tm, tn), lambda i, j: (i, j))],
        out_specs=pl.BlockSpec((tm, tn), lambda i, j: (i, j)),
        out_shape=jax.ShapeDtypeStruct((t, d), F32),
        compiler_params=_params("parallel", "arbitrary"),
        name="out_proj",
    )(attn, conv, w, w, x2d)


def _layer(x, norm_gain, w_in, q_gain, k_gain, sink, dw_w, dw_b, ln_g, ln_b, pw_w, pw_b, w_out):
    batch, seq, d = x.shape
    t = batch * seq
    qw = N_Q_HEADS * HEAD_DIM
    kvw = N_KV_HEADS * HEAD_DIM
    cw = dw_w.shape[1]
    x2d = x.reshape(t, d)

    c_v = qw + kvw
    c_ga = c_v + kvw
    c_ua = c_ga + qw
    c_ub = c_ua + cw
    c_gc = c_ub + cw
    w_bf = w_in.astype(BF16)

    cos_t, sa_t, sb_t = _rope_tables(seq)
    qg = (q_gain.astype(F32) * (LOG2E / math.sqrt(HEAD_DIM))).reshape(1, HEAD_DIM)
    kg = k_gain.astype(F32).reshape(1, HEAD_DIM)
    xn, kv = _norm_kv_proj(x2d, norm_gain, w_bf, qw // (2 * kvw), cos_t, sa_t, sb_t, kg, seq)

    tm = 1024
    rope_spec = pl.BlockSpec((tm, HEAD_DIM), lambda i, j: (i % (seq // tm), 0))
    gain_spec = pl.BlockSpec((1, HEAD_DIM), lambda i, j: (0, 0))
    tn_q = 1024
    q = _proj_call(_q_kernel, xn, w_bf, [lambda j: j], [cos_t, sa_t, sb_t, qg],
                   [rope_spec] * 3 + [gain_spec], tm, tn_q, qw // tn_q, "proj_q")
    tn_g = 1024
    n_ga = qw // tn_g
    gates = _proj_call(_silu_kernel, xn, w_bf,
                       [lambda j: jnp.where(j < n_ga, c_ga // tn_g + j, c_gc // tn_g + j - n_ga)],
                       [], [], tm, tn_g, (qw + cw) // tn_g, "proj_gates")
    tn_u = 512
    hglu = _proj_call(_glu_kernel, xn, w_bf,
                      [lambda j: c_ua // tn_u + j, lambda j: c_ub // tn_u + j],
                      [], [], tm, tn_u, cw // tn_u, "proj_glu")

    attn = _attention(q, kv, gates, sink.astype(F32), batch, seq)
    conv = _conformer_conv(hglu, gates, dw_w, dw_b, ln_g, ln_b, pw_w.astype(BF16), pw_b, batch, seq)

    out = _out_proj(attn, conv, w_out.astype(BF16), x2d)
    return out.reshape(batch, seq, d)


def kernel(x, norm_gain, w_in, q_norm_gain, k_norm_gain, attn_sink, conv_dw_w, conv_dw_b,
           conv_ln_gain, conv_ln_bias, conv_pw_w, conv_pw_b, w_out):
    depth = norm_gain.shape[0]
    for layer in range(depth):
        x = _layer(x, norm_gain[layer], w_in[layer], q_norm_gain[layer], k_norm_gain[layer],
                   attn_sink[layer], conv_dw_w[layer], conv_dw_b[layer], conv_ln_gain[layer],
                   conv_ln_bias[layer], conv_pw_w[layer], conv_pw_b[layer], w_out[layer])
    return x
```

```python
import functools
import math

import jax
import jax.numpy as jnp
from jax import lax
from jax.experimental import pallas as pl
from jax.experimental.pallas import tpu as pltpu

F32 = jnp.float32
BF16 = jnp.bfloat16

HEAD_DIM = 128
N_Q_HEADS = 16
N_KV_HEADS = 4
GQA_GROUP = N_Q_HEADS // N_KV_HEADS
ATTN_BLOCK = 128
WINDOW = 128
ROPE_THETA = 500000.0
ROPE_DIM = HEAD_DIM // 4
ROPE_HALF = ROPE_DIM // 2
CONV_TAPS = 31
CONV_PAD = CONV_TAPS // 2
NORM_EPS = 1e-6
LN_EPS = 1e-5
MASK_VALUE = -1e30
LOG2E = math.log2(math.e)
ATTN_BLOCKS_PER_STEP = 4
EPILOGUE_ROWS = 256
NORM_ROWS = 64

LANES = 128
SUBLANES = 8
MXU_COLS = 256
CONV_ROW_BLOCK = 64
CONV_SHIFT_WINDOW = 128
POINTWISE_ROWS = 256
BF16_SUBLANE_TILE = 16
VMEM_LIMIT_BYTES = 56 * 1024 * 1024


def _params(*sem):
    return pltpu.CompilerParams(dimension_semantics=sem, vmem_limit_bytes=VMEM_LIMIT_BYTES)


def _qk_tile(xn_ref, w_ref, cos_ref, sa_ref, sb_ref, gain_ref, o_ref, n_norm_heads):
    gain = gain_ref[...]
    rest = n_norm_heads * HEAD_DIM
    for r0 in range(0, xn_ref.shape[0], EPILOGUE_ROWS):
        rows = slice(r0, r0 + EPILOGUE_ROWS)
        acc = jnp.dot(xn_ref[rows, :], w_ref[...], preferred_element_type=F32)
        cos, sa, sb = cos_ref[rows, :], sa_ref[rows, :], sb_ref[rows, :]
        for h in range(n_norm_heads):
            xh = acc[:, h * HEAD_DIM:(h + 1) * HEAD_DIM]
            ms = jnp.mean(xh * xh, axis=-1, keepdims=True)
            y = xh * lax.rsqrt(ms + NORM_EPS) * gain
            y = (y * cos + pltpu.roll(y, HEAD_DIM - ROPE_HALF, 1) * sa
                 + pltpu.roll(y, ROPE_HALF, 1) * sb)
            o_ref[rows, h * HEAD_DIM:(h + 1) * HEAD_DIM] = y.astype(o_ref.dtype)
        if rest < acc.shape[1]:
            o_ref[rows, rest:] = acc[:, rest:].astype(o_ref.dtype)


def _q_kernel(xn_ref, w_ref, cos_ref, sa_ref, sb_ref, gain_ref, o_ref):
    _qk_tile(xn_ref, w_ref, cos_ref, sa_ref, sb_ref, gain_ref, o_ref, o_ref.shape[1] // HEAD_DIM)


def _norm_kv_kernel(x_ref, ng_ref, w_ref, cos_ref, sa_ref, sb_ref, gain_ref, xn_ref, kv_ref):
    ng = ng_ref[...]
    for r0 in range(0, x_ref.shape[0], NORM_ROWS):
        rows = slice(r0, r0 + NORM_ROWS)
        x = x_ref[rows, :]
        ms = jnp.mean(x * x, axis=-1, keepdims=True)
        xn_ref[rows, :] = (x * lax.rsqrt(ms + NORM_EPS) * ng).astype(xn_ref.dtype)
    _qk_tile(xn_ref, w_ref, cos_ref, sa_ref, sb_ref, gain_ref, kv_ref, N_KV_HEADS)


def _norm_kv_proj(x2d, norm_gain, w, w_col_block, cos_t, sa_t, sb_t, k_gain, seq, tm=512):
    t, d = x2d.shape
    tn = 2 * N_KV_HEADS * HEAD_DIM
    rope_spec = pl.BlockSpec((tm, HEAD_DIM), lambda i: (i % (seq // tm), 0))
    once = dict(pipeline_mode=pl.Buffered(1))
    return pl.pallas_call(
        _norm_kv_kernel,
        grid=(t // tm,),
        in_specs=[pl.BlockSpec((tm, d), lambda i: (i, 0)),
                  pl.BlockSpec((1, d), lambda i: (0, 0), **once),
                  pl.BlockSpec((d, tn), lambda i: (0, w_col_block), **once),
                  rope_spec, rope_spec, rope_spec,
                  pl.BlockSpec((1, HEAD_DIM), lambda i: (0, 0), **once)],
        out_specs=[pl.BlockSpec((tm, d), lambda i: (i, 0)),
                   pl.BlockSpec((tm, tn), lambda i: (i, 0))],
        out_shape=[jax.ShapeDtypeStruct((t, d), BF16),
                   jax.ShapeDtypeStruct((t, tn), BF16)],
        compiler_params=_params("parallel"),
        name="norm_proj_kv",
    )(x2d, norm_gain.reshape(1, d), w, cos_t, sa_t, sb_t, k_gain)


def _silu_kernel(xn_ref, w_ref, o_ref):
    for r0 in range(0, xn_ref.shape[0], EPILOGUE_ROWS):
        rows = slice(r0, r0 + EPILOGUE_ROWS)
        acc = jnp.dot(xn_ref[rows, :], w_ref[...], preferred_element_type=F32)
        o_ref[rows, :] = (acc * jax.nn.sigmoid(acc)).astype(o_ref.dtype)


def _glu_kernel(xn_ref, wa_ref, wb_ref, o_ref):
    for r0 in range(0, xn_ref.shape[0], EPILOGUE_ROWS):
        rows = slice(r0, r0 + EPILOGUE_ROWS)
        xn = xn_ref[rows, :]
        a = jnp.dot(xn, wa_ref[...], preferred_element_type=F32)
        b = jnp.dot(xn, wb_ref[...], preferred_element_type=F32)
        o_ref[rows, :] = (a * jax.nn.sigmoid(b)).astype(o_ref.dtype)


def _proj_call(kernel, xn, w, w_col_blocks, extra_inputs, extra_specs, tm, tn, n_tiles, name):
    t, d = xn.shape
    w_specs = [pl.BlockSpec((d, tn), functools.partial(lambda i, j, f: (0, f(j)), f=f))
               for f in w_col_blocks]
    return pl.pallas_call(
        kernel,
        grid=(t // tm, n_tiles),
        in_specs=[pl.BlockSpec((tm, d), lambda i, j: (i, 0))] + w_specs + extra_specs,
        out_specs=pl.BlockSpec((tm, tn), lambda i, j: (i, j)),
        out_shape=jax.ShapeDtypeStruct((t, n_tiles * tn), BF16),
        compiler_params=_params("parallel", "arbitrary"),
        name=name,
    )(xn, *([w] * len(w_col_blocks)), *extra_inputs)


def _rope_tables(seq):
    inv_freq = jnp.power(jnp.float32(ROPE_THETA),
                         -jnp.arange(ROPE_HALF, dtype=F32) * (2.0 / ROPE_DIM))
    ang = jnp.arange(seq).astype(F32)[:, None] * inv_freq[None, :]
    cos, sin = jnp.cos(ang), jnp.sin(ang)
    zeros = jnp.zeros((seq, HEAD_DIM - ROPE_DIM), F32)
    zh = jnp.zeros((seq, ROPE_HALF), F32)
    cos_t = jnp.concatenate([cos, cos, jnp.ones_like(zeros)], axis=-1)
    sa_t = jnp.concatenate([-sin, zh, zeros], axis=-1)
    sb_t = jnp.concatenate([zh, sin, zeros], axis=-1)
    return cos_t, sa_t, sb_t


def _attn_kernel(sink_ref, q_ref, kvp_ref, kvc_ref, kvn_ref, g_ref, o_ref, *, n_steps):
    step = pl.program_id(1)
    blk = ATTN_BLOCK
    kvw = N_KV_HEADS * HEAD_DIM
    kv = jnp.concatenate([kvp_ref[...], kvc_ref[...], kvn_ref[...]], axis=0)
    key = lax.broadcasted_iota(jnp.int32, (3 * blk, blk), 0)
    qry = lax.broadcasted_iota(jnp.int32, (3 * blk, blk), 1)
    in_band = jnp.abs(key - blk - qry) <= WINDOW
    lane_head = lax.broadcasted_iota(jnp.int32, (1, GQA_GROUP * blk), 1) // blk
    for u in range(ATTN_BLOCKS_PER_STEP):
        valid = in_band
        if u == 0:
            valid = valid & ((key >= blk) | (step > 0))
        if u == ATTN_BLOCKS_PER_STEP - 1:
            valid = valid & ((key < 2 * blk) | (step < n_steps - 1))
        bias1 = jnp.where(valid, 0.0, MASK_VALUE).astype(F32)
        bias = jnp.concatenate([bias1] * GQA_GROUP, axis=1)
        rows = slice(u * blk, (u + 1) * blk)
        for h in range(N_KV_HEADS):
            qs = jnp.concatenate(
                [q_ref[rows, (h * GQA_GROUP + g) * HEAD_DIM:(h * GQA_GROUP + g + 1) * HEAD_DIM]
                 for g in range(GQA_GROUP)], axis=0)
            kh = kv[u * blk:(u + 3) * blk, h * HEAD_DIM:(h + 1) * HEAD_DIM]
            vh = kv[u * blk:(u + 3) * blk, kvw + h * HEAD_DIM:kvw + (h + 1) * HEAD_DIM]
            s = lax.dot_general(kh, qs, (((1,), (1,)), ((), ())),
                                preferred_element_type=F32) + bias
            sink = jnp.zeros((1, GQA_GROUP * blk), F32)
            for g in range(GQA_GROUP):
                sink = jnp.where(lane_head == g, sink_ref[h * GQA_GROUP + g] * LOG2E, sink)
            m = jnp.maximum(jnp.max(s, axis=0, keepdims=True), sink)
            p = jnp.exp2(s - m)
            den = jnp.sum(p, axis=0, keepdims=True) + jnp.exp2(sink - m)
            ot = lax.dot_general(vh, p.astype(BF16), (((0,), (0,)), ((), ())),
                                 preferred_element_type=F32)
            ot = ot / den
            for g in range(GQA_GROUP):
                c0 = (h * GQA_GROUP + g) * HEAD_DIM
                gate = g_ref[rows, c0:c0 + HEAD_DIM].astype(F32)
                o = ot[:, g * blk:(g + 1) * blk].T
                o_ref[rows, c0:c0 + HEAD_DIM] = (o * gate).astype(o_ref.dtype)


def _attention(q, kv, gates, sink, batch, seq):
    t, qw = q.shape
    blk = ATTN_BLOCK
    u = ATTN_BLOCKS_PER_STEP
    nb = seq // blk
    ns = nb // u
    kv2 = kv.shape[1]
    return pl.pallas_call(
        functools.partial(_attn_kernel, n_steps=ns),
        grid=(batch, ns),
        in_specs=[pl.BlockSpec(memory_space=pltpu.SMEM),
                  pl.BlockSpec((u * blk, qw), lambda b, s: (b * ns + s, 0)),
                  pl.BlockSpec((blk, kv2), lambda b, s: (b * nb + jnp.maximum(s * u - 1, 0), 0)),
                  pl.BlockSpec((u * blk, kv2), lambda b, s: (b * ns + s, 0)),
                  pl.BlockSpec((blk, kv2), lambda b, s: (b * nb + jnp.minimum(s * u + u, nb - 1), 0)),
                  pl.BlockSpec((u * blk, qw), lambda b, s: (b * ns + s, 0))],
        out_specs=pl.BlockSpec((u * blk, qw), lambda b, s: (b * ns + s, 0)),
        out_shape=jax.ShapeDtypeStruct((t, qw), BF16),
        compiler_params=_params("parallel", "arbitrary"),
        name="band_attention",
    )(sink, q, kv, kv, kv, gates)


def _conv_shift_geometry():
    base = BF16_SUBLANE_TILE - CONV_PAD
    span = CONV_ROW_BLOCK + ((base + CONV_TAPS - 1) // SUBLANES) * SUBLANES
    return base, span


def _conv_shift_matrix():
    _, span = _conv_shift_geometry()
    i = jnp.arange(SUBLANES * span)
    sel = (i % span + i // span)[:, None] == jnp.arange(CONV_SHIFT_WINDOW)[None, :]
    return sel.astype(BF16)


def _conv_kernel(hp_ref, hc_ref, hn_ref, shift_ref, dww_ref, dwb_ref, lng_ref, lnb_ref, pww_ref,
                 pwb_ref, g_ref, o_ref, hbuf, cbuf, *, n_tiles):
    i = pl.program_id(1)
    ts = hc_ref.shape[0]
    halo = BF16_SUBLANE_TILE
    width = hc_ref.shape[1]
    hbuf[0:halo, :] = jnp.where(i > 0, hp_ref[...], jnp.zeros_like(hp_ref[...]))
    hbuf[halo:halo + ts, :] = hc_ref[...]
    hbuf[halo + ts:2 * halo + ts, :] = jnp.where(i < n_tiles - 1, hn_ref[...],
                                                 jnp.zeros_like(hn_ref[...]))
    hbuf[2 * halo + ts:, :] = jnp.zeros((hbuf.shape[0] - 2 * halo - ts, width), hbuf.dtype)
    base, span = _conv_shift_geometry()
    rb = CONV_ROW_BLOCK
    for r0 in range(0, ts, rb):
        for n0 in range(0, width, MXU_COLS):
            window = hbuf[r0:r0 + CONV_SHIFT_WINDOW, n0:n0 + MXU_COLS]
            lanes = [slice(n0 + c0, n0 + c0 + LANES) for c0 in range(0, MXU_COLS, LANES)]
            accs = [jnp.broadcast_to(dwb_ref[:, cs], (rb, LANES)) for cs in lanes]
            for b in range(SUBLANES):
                shifted = jnp.dot(shift_ref[b * span:(b + 1) * span, :], window,
                                  preferred_element_type=F32)
                for t in range(CONV_TAPS):
                    if (base + t) % SUBLANES == b:
                        a8 = base + t - b
                        for k, cs in enumerate(lanes):
                            accs[k] = accs[k] + (shifted[a8:a8 + rb, k * LANES:(k + 1) * LANES]
                                                 * dww_ref[t:t + 1, cs])
            for k, cs in enumerate(lanes):
                cbuf[r0:r0 + rb, cs] = accs[k]
        if (r0 + rb) % POINTWISE_ROWS == 0:
            rows = slice(r0 + rb - POINTWISE_ROWS, r0 + rb)
            y = cbuf[rows, :]
            mu = jnp.mean(y, axis=-1, keepdims=True)
            yc = y - mu
            var = jnp.mean(yc * yc, axis=-1, keepdims=True)
            yn = yc * lax.rsqrt(var + LN_EPS) * lng_ref[...] + lnb_ref[...]
            z = (yn * jax.nn.sigmoid(yn)).astype(BF16)
            o = jnp.dot(z, pww_ref[...], preferred_element_type=F32) + pwb_ref[...]
            o_ref[rows, :] = (o * g_ref[rows, :].astype(F32)).astype(o_ref.dtype)


def _conformer_conv(hglu, gates, dw_w, dw_b, ln_g, ln_b, pw_w, pw_b, batch, seq, ts=512):
    t, width = hglu.shape
    nt = seq // ts
    halo = BF16_SUBLANE_TILE
    r = ts // halo
    nh = seq // halo
    row = lambda v: v.reshape(1, width)
    const = lambda b, i: (0, 0)
    once = dict(pipeline_mode=pl.Buffered(1))
    shift = _conv_shift_matrix()
    return pl.pallas_call(
        functools.partial(_conv_kernel, n_tiles=nt),
        grid=(batch, nt),
        in_specs=[pl.BlockSpec((halo, width), lambda b, i: (b * nh + jnp.maximum(i * r - 1, 0), 0)),
                  pl.BlockSpec((ts, width), lambda b, i: (b * nt + i, 0)),
                  pl.BlockSpec((halo, width), lambda b, i: (b * nh + jnp.minimum((i + 1) * r, nh - 1), 0)),
                  pl.BlockSpec(shift.shape, const, **once),
                  pl.BlockSpec((CONV_TAPS, width), const, **once),
                  pl.BlockSpec((1, width), const, **once),
                  pl.BlockSpec((1, width), const, **once),
                  pl.BlockSpec((1, width), const, **once),
                  pl.BlockSpec((width, width), const, **once),
                  pl.BlockSpec((1, width), const, **once),
                  pl.BlockSpec((ts, width), lambda b, i: (b * nt + i, 1))],
        out_specs=pl.BlockSpec((ts, width), lambda b, i: (b * nt + i, 0)),
        out_shape=jax.ShapeDtypeStruct((t, width), BF16),
        scratch_shapes=[pltpu.VMEM((ts - CONV_ROW_BLOCK + CONV_SHIFT_WINDOW, width), BF16),
                        pltpu.VMEM((ts, width), F32)],
        compiler_params=_params("parallel", "arbitrary"),
        name="conformer_conv",
    )(hglu, hglu, hglu, shift, dw_w, row(dw_b), row(ln_g), row(ln_b), pw_w, row(pw_b), gates)


def _out_kernel(a_ref, c_ref, wa_ref, wc_ref, x_ref, o_ref):
    acc = jnp.dot(a_ref[...], wa_ref[...], preferred_element_type=F32)
    acc = acc + jnp.dot(c_ref[...], wc_ref[...], preferred_element_type=F32)
    o_ref[...] = x_ref[...] + acc


def _out_proj(attn, conv, w, x2d, tm=1024, tn=1024):
    t, d = x2d.shape
    ka, kc = attn.shape[1], conv.shape[1]
    assert ka == kc and w.shape[0] == ka + kc
    return pl.pallas_call(
        _out_kernel,
        grid=(t // tm, d // tn),
        in_specs=[pl.BlockSpec((tm, ka), lambda i, j: (i, 0)),
                  pl.BlockSpec((tm, kc), lambda i, j: (i, 0)),
                  pl.BlockSpec((ka, tn), lambda i, j: (0, j)),
                  pl.BlockSpec((kc, tn), lambda i, j: (1, j)),
                  pl.BlockSpec((tm, tn), lambda i, j: (i, j))],
        out_specs=pl.BlockSpec((tm, tn), lambda i, j: (i, j)),
        out_shape=jax.ShapeDtypeStruct((t, d), F32),
        compiler_params=_params("parallel", "arbitrary"),
        name="out_proj",
    )(attn, conv, w, w, x2d)


def _layer(x, norm_gain, w_in, q_gain, k_gain, sink, dw_w, dw_b, ln_g, ln_b, pw_w, pw_b, w_out):
    batch, seq, d = x.shape
    t = batch * seq
    qw = N_Q_HEADS * HEAD_DIM
    kvw = N_KV_HEADS * HEAD_DIM
    cw = dw_w.shape[1]
    x2d = x.reshape(t, d)

    c_v = qw + kvw
    c_ga = c_v + kvw
    c_ua = c_ga + qw
    c_ub = c_ua + cw
    c_gc = c_ub + cw
    w_bf = w_in.astype(BF16)

    cos_t, sa_t, sb_t = _rope_tables(seq)
    qg = (q_gain.astype(F32) * (LOG2E / math.sqrt(HEAD_DIM))).reshape(1, HEAD_DIM)
    kg = k_gain.astype(F32).reshape(1, HEAD_DIM)
    xn, kv = _norm_kv_proj(x2d, norm_gain, w_bf, qw // (2 * kvw), cos_t, sa_t, sb_t, kg, seq)

    tm = 1024
    rope_spec = pl.BlockSpec((tm, HEAD_DIM), lambda i, j: (i % (seq // tm), 0))
    gain_spec = pl.BlockSpec((1, HEAD_DIM), lambda i, j: (0, 0))
    tn_q = 1024
    q = _proj_call(_q_kernel, xn, w_bf, [lambda j: j], [cos_t, sa_t, sb_t, qg],
                   [rope_spec] * 3 + [gain_spec], tm, tn_q, qw // tn_q, "proj_q")
    tn_g = 1024
    n_ga = qw // tn_g
    gates = _proj_call(_silu_kernel, xn, w_bf,
                       [lambda j: jnp.where(j < n_ga, c_ga // tn_g + j, c_gc // tn_g + j - n_ga)],
                       [], [], tm, tn_g, (qw + cw) // tn_g, "proj_gates")
    tn_u = 512
    hglu = _proj_call(_glu_kernel, xn, w_bf,
                      [lambda j: c_ua // tn_u + j, lambda j: c_ub // tn_u + j],
                      [], [], tm, tn_u, cw // tn_u, "proj_glu")

    attn = _attention(q, kv, gates, sink.astype(F32), batch, seq)
    conv = _conformer_conv(hglu, gates, dw_w, dw_b, ln_g, ln_b, pw_w.astype(BF16), pw_b, batch, seq)

    out = _out_proj(attn, conv, w_out.astype(BF16), x2d)
    return out.reshape(batch, seq, d)


def kernel(x, norm_gain, w_in, q_norm_gain, k_norm_gain, attn_sink, conv_dw_w, conv_dw_b,
           conv_ln_gain, conv_ln_bias, conv_pw_w, conv_pw_b, w_out):
    depth = norm_gain.shape[0]
    for layer in range(depth):
        x = _layer(x, norm_gain[layer], w_in[layer], q_norm_gain[layer], k_norm_gain[layer],
                   attn_sink[layer], conv_dw_w[layer], conv_dw_b[layer], conv_ln_gain[layer],
                   conv_ln_bias[layer], conv_pw_w[layer], conv_pw_b[layer], w_out[layer])
    return x
```

```python
import functools
import math

import jax
import jax.numpy as jnp
from jax import lax
from jax.experimental import pallas as pl
from jax.experimental.pallas import tpu as pltpu

F32 = jnp.float32
BF16 = jnp.bfloat16

HEAD_DIM = 128
N_Q_HEADS = 16
N_KV_HEADS = 4
GQA_GROUP = N_Q_HEADS // N_KV_HEADS
ATTN_BLOCK = 128
WINDOW = 128
ROPE_THETA = 500000.0
ROPE_DIM = HEAD_DIM // 4
ROPE_HALF = ROPE_DIM // 2
CONV_TAPS = 31
CONV_PAD = CONV_TAPS // 2
NORM_EPS = 1e-6
LN_EPS = 1e-5
MASK_VALUE = -1e30
LOG2E = math.log2(math.e)
ATTN_BLOCKS_PER_STEP = 4
EPILOGUE_ROWS = 256
NORM_ROWS = 64

LANES = 128
SUBLANES = 8
CONV_CHAIN_ROWS = 16
CONV_PARTIAL_SUMS = 2
BF16_SUBLANE_TILE = 16
VMEM_LIMIT_BYTES = 56 * 1024 * 1024


def _params(*sem):
    return pltpu.CompilerParams(dimension_semantics=sem, vmem_limit_bytes=VMEM_LIMIT_BYTES)


def _qk_tile(xn_ref, w_ref, cos_ref, sa_ref, sb_ref, gain_ref, o_ref, n_norm_heads):
    gain = gain_ref[...]
    rest = n_norm_heads * HEAD_DIM
    for r0 in range(0, xn_ref.shape[0], EPILOGUE_ROWS):
        rows = slice(r0, r0 + EPILOGUE_ROWS)
        acc = jnp.dot(xn_ref[rows, :], w_ref[...], preferred_element_type=F32)
        cos, sa, sb = cos_ref[rows, :], sa_ref[rows, :], sb_ref[rows, :]
        for h in range(n_norm_heads):
            xh = acc[:, h * HEAD_DIM:(h + 1) * HEAD_DIM]
            ms = jnp.mean(xh * xh, axis=-1, keepdims=True)
            y = xh * lax.rsqrt(ms + NORM_EPS) * gain
            y = (y * cos + pltpu.roll(y, HEAD_DIM - ROPE_HALF, 1) * sa
                 + pltpu.roll(y, ROPE_HALF, 1) * sb)
            o_ref[rows, h * HEAD_DIM:(h + 1) * HEAD_DIM] = y.astype(o_ref.dtype)
        if rest < acc.shape[1]:
            o_ref[rows, rest:] = acc[:, rest:].astype(o_ref.dtype)


def _q_kernel(xn_ref, w_ref, cos_ref, sa_ref, sb_ref, gain_ref, o_ref):
    _qk_tile(xn_ref, w_ref, cos_ref, sa_ref, sb_ref, gain_ref, o_ref, o_ref.shape[1] // HEAD_DIM)


def _norm_kv_kernel(x_ref, ng_ref, w_ref, cos_ref, sa_ref, sb_ref, gain_ref, xn_ref, kv_ref):
    ng = ng_ref[...]
    for r0 in range(0, x_ref.shape[0], NORM_ROWS):
        rows = slice(r0, r0 + NORM_ROWS)
        x = x_ref[rows, :]
        ms = jnp.mean(x * x, axis=-1, keepdims=True)
        xn_ref[rows, :] = (x * lax.rsqrt(ms + NORM_EPS) * ng).astype(xn_ref.dtype)
    _qk_tile(xn_ref, w_ref, cos_ref, sa_ref, sb_ref, gain_ref, kv_ref, N_KV_HEADS)


def _norm_kv_proj(x2d, norm_gain, w, w_col_block, cos_t, sa_t, sb_t, k_gain, seq, tm=512):
    t, d = x2d.shape
    tn = 2 * N_KV_HEADS * HEAD_DIM
    rope_spec = pl.BlockSpec((tm, HEAD_DIM), lambda i: (i % (seq // tm), 0))
    once = dict(pipeline_mode=pl.Buffered(1))
    return pl.pallas_call(
        _norm_kv_kernel,
        grid=(t // tm,),
        in_specs=[pl.BlockSpec((tm, d), lambda i: (i, 0)),
                  pl.BlockSpec((1, d), lambda i: (0, 0), **once),
                  pl.BlockSpec((d, tn), lambda i: (0, w_col_block), **once),
                  rope_spec, rope_spec, rope_spec,
                  pl.BlockSpec((1, HEAD_DIM), lambda i: (0, 0), **once)],
        out_specs=[pl.BlockSpec((tm, d), lambda i: (i, 0)),
                   pl.BlockSpec((tm, tn), lambda i: (i, 0))],
        out_shape=[jax.ShapeDtypeStruct((t, d), BF16),
                   jax.ShapeDtypeStruct((t, tn), BF16)],
        compiler_params=_params("parallel"),
        name="norm_proj_kv",
    )(x2d, norm_gain.reshape(1, d), w, cos_t, sa_t, sb_t, k_gain)


def _silu_kernel(xn_ref, w_ref, o_ref):
    for r0 in range(0, xn_ref.shape[0], EPILOGUE_ROWS):
        rows = slice(r0, r0 + EPILOGUE_ROWS)
        acc = jnp.dot(xn_ref[rows, :], w_ref[...], preferred_element_type=F32)
        o_ref[rows, :] = (acc * jax.nn.sigmoid(acc)).astype(o_ref.dtype)


def _glu_kernel(xn_ref, wa_ref, wb_ref, o_ref):
    for r0 in range(0, xn_ref.shape[0], EPILOGUE_ROWS):
        rows = slice(r0, r0 + EPILOGUE_ROWS)
        xn = xn_ref[rows, :]
        a = jnp.dot(xn, wa_ref[...], preferred_element_type=F32)
        b = jnp.dot(xn, wb_ref[...], preferred_element_type=F32)
        o_ref[rows, :] = (a * jax.nn.sigmoid(b)).astype(o_ref.dtype)


def _proj_call(kernel, xn, w, w_col_blocks, extra_inputs, extra_specs, tm, tn, n_tiles, name):
    t, d = xn.shape
    w_specs = [pl.BlockSpec((d, tn), functools.partial(lambda i, j, f: (0, f(j)), f=f))
               for f in w_col_blocks]
    return pl.pallas_call(
        kernel,
        grid=(t // tm, n_tiles),
        in_specs=[pl.BlockSpec((tm, d), lambda i, j: (i, 0))] + w_specs + extra_specs,
        out_specs=pl.BlockSpec((tm, tn), lambda i, j: (i, j)),
        out_shape=jax.ShapeDtypeStruct((t, n_tiles * tn), BF16),
        compiler_params=_params("parallel", "arbitrary"),
        name=name,
    )(xn, *([w] * len(w_col_blocks)), *extra_inputs)


def _rope_tables(seq):
    inv_freq = jnp.power(jnp.float32(ROPE_THETA),
                         -jnp.arange(ROPE_HALF, dtype=F32) * (2.0 / ROPE_DIM))
    ang = jnp.arange(seq).astype(F32)[:, None] * inv_freq[None, :]
    cos, sin = jnp.cos(ang), jnp.sin(ang)
    zeros = jnp.zeros((seq, HEAD_DIM - ROPE_DIM), F32)
    zh = jnp.zeros((seq, ROPE_HALF), F32)
    cos_t = jnp.concatenate([cos, cos, jnp.ones_like(zeros)], axis=-1)
    sa_t = jnp.concatenate([-sin, zh, zeros], axis=-1)
    sb_t = jnp.concatenate([zh, sin, zeros], axis=-1)
    return cos_t, sa_t, sb_t


def _attn_kernel(sink_ref, q_ref, kvp_ref, kvc_ref, kvn_ref, g_ref, o_ref, *, n_steps):
    step = pl.program_id(1)
    blk = ATTN_BLOCK
    kvw = N_KV_HEADS * HEAD_DIM
    kv = jnp.concatenate([kvp_ref[...], kvc_ref[...], kvn_ref[...]], axis=0)
    key = lax.broadcasted_iota(jnp.int32, (3 * blk, blk), 0)
    qry = lax.broadcasted_iota(jnp.int32, (3 * blk, blk), 1)
    in_band = jnp.abs(key - blk - qry) <= WINDOW
    lane_head = lax.broadcasted_iota(jnp.int32, (1, GQA_GROUP * blk), 1) // blk
    for u in range(ATTN_BLOCKS_PER_STEP):
        valid = in_band
        if u == 0:
            valid = valid & ((key >= blk) | (step > 0))
        if u == ATTN_BLOCKS_PER_STEP - 1:
            valid = valid & ((key < 2 * blk) | (step < n_steps - 1))
        bias1 = jnp.where(valid, 0.0, MASK_VALUE).astype(F32)
        bias = jnp.concatenate([bias1] * GQA_GROUP, axis=1)
        rows = slice(u * blk, (u + 1) * blk)
        for h in range(N_KV_HEADS):
            qs = jnp.concatenate(
                [q_ref[rows, (h * GQA_GROUP + g) * HEAD_DIM:(h * GQA_GROUP + g + 1) * HEAD_DIM]
                 for g in range(GQA_GROUP)], axis=0)
            kh = kv[u * blk:(u + 3) * blk, h * HEAD_DIM:(h + 1) * HEAD_DIM]
            vh = kv[u * blk:(u + 3) * blk, kvw + h * HEAD_DIM:kvw + (h + 1) * HEAD_DIM]
            s = lax.dot_general(kh, qs, (((1,), (1,)), ((), ())),
                                preferred_element_type=F32) + bias
            sink = jnp.zeros((1, GQA_GROUP * blk), F32)
            for g in range(GQA_GROUP):
                sink = jnp.where(lane_head == g, sink_ref[h * GQA_GROUP + g] * LOG2E, sink)
            m = jnp.maximum(jnp.max(s, axis=0, keepdims=True), sink)
            p = jnp.exp2(s - m)
            den = jnp.sum(p, axis=0, keepdims=True) + jnp.exp2(sink - m)
            ot = lax.dot_general(vh, p.astype(BF16), (((0,), (0,)), ((), ())),
                                 preferred_element_type=F32)
            ot = ot / den
            for g in range(GQA_GROUP):
                c0 = (h * GQA_GROUP + g) * HEAD_DIM
                gate = g_ref[rows, c0:c0 + HEAD_DIM].astype(F32)
                o = ot[:, g * blk:(g + 1) * blk].T
                o_ref[rows, c0:c0 + HEAD_DIM] = (o * gate).astype(o_ref.dtype)


def _attention(q, kv, gates, sink, batch, seq):
    t, qw = q.shape
    blk = ATTN_BLOCK
    u = ATTN_BLOCKS_PER_STEP
    nb = seq // blk
    ns = nb // u
    kv2 = kv.shape[1]
    return pl.pallas_call(
        functools.partial(_attn_kernel, n_steps=ns),
        grid=(batch, ns),
        in_specs=[pl.BlockSpec(memory_space=pltpu.SMEM),
                  pl.BlockSpec((u * blk, qw), lambda b, s: (b * ns + s, 0)),
                  pl.BlockSpec((blk, kv2), lambda b, s: (b * nb + jnp.maximum(s * u - 1, 0), 0)),
                  pl.BlockSpec((u * blk, kv2), lambda b, s: (b * ns + s, 0)),
                  pl.BlockSpec((blk, kv2), lambda b, s: (b * nb + jnp.minimum(s * u + u, nb - 1), 0)),
                  pl.BlockSpec((u * blk, qw), lambda b, s: (b * ns + s, 0))],
        out_specs=pl.BlockSpec((u * blk, qw), lambda b, s: (b * ns + s, 0)),
        out_shape=jax.ShapeDtypeStruct((t, qw), BF16),
        compiler_params=_params("parallel", "arbitrary"),
        name="band_attention",
    )(sink, q, kv, kv, kv, gates)


def _conv_load_rows(hp_ref, hc_ref, hn_ref, hbuf, first, last):
    rb = hc_ref.shape[0]
    halo = BF16_SUBLANE_TILE
    hbuf[0:halo, :] = jnp.where(first, 0.0, hp_ref[...].astype(F32))
    hbuf[halo:halo + rb, :] = hc_ref[...].astype(F32)
    hbuf[halo + rb:, :] = jnp.where(last, 0.0, hn_ref[...].astype(F32))


def _zero_bits_after(v):
    u = lax.bitcast_convert_type(v[:SUBLANES], jnp.uint32)
    return lax.shift_right_logical(lax.shift_right_logical(u, jnp.uint32(16)), jnp.uint32(16))


def _conv_taps(hbuf, cbuf, dww_ref, dwb_ref, c, token):
    rb = cbuf.shape[0]
    base = BF16_SUBLANE_TILE - CONV_PAD
    span = rb + ((base + CONV_TAPS - 1) // SUBLANES) * SUBLANES
    cs = slice(c * LANES, (c + 1) * LANES)
    rows = hbuf[0:span + SUBLANES, cs]
    shifted = [rows] + [pltpu.roll(rows, span + SUBLANES - b, 0) for b in range(1, SUBLANES)]
    gr = CONV_CHAIN_ROWS
    bias = lax.bitcast_convert_type(jnp.broadcast_to(dwb_ref[:, cs], (gr, LANES)), jnp.uint32)
    for g0 in range(0, rb, gr):
        acc = lax.bitcast_convert_type(
            bias | jnp.concatenate([token] * (gr // SUBLANES), axis=0), F32)
        partial = [acc] + [None] * (CONV_PARTIAL_SUMS - 1)
        for t in range(CONV_TAPS):
            b = (base + t) % SUBLANES
            a8 = base + t - b + g0
            term = shifted[b][a8:a8 + gr] * dww_ref[t:t + 1, cs]
            k = t % CONV_PARTIAL_SUMS
            partial[k] = term if partial[k] is None else partial[k] + term
        acc = functools.reduce(lambda u, v: u + v, partial)
        cbuf[g0:g0 + gr, cs] = acc
        token = _zero_bits_after(acc)
    return token


def _conv_finish(cbuf, g_ref, lng_ref, lnb_ref, pww_ref, pwb_ref):
    y = cbuf[...]
    mu = jnp.mean(y, axis=-1, keepdims=True)
    yc = y - mu
    var = jnp.mean(yc * yc, axis=-1, keepdims=True)
    yn = yc * lax.rsqrt(var + LN_EPS) * lng_ref[...] + lnb_ref[...]
    z = (yn * jax.nn.sigmoid(yn)).astype(BF16)
    o = jnp.dot(z, pww_ref[...], preferred_element_type=F32) + pwb_ref[...]
    return o * g_ref[...].astype(F32)


def _conv_out_kernel(hp_ref, hc_ref, hn_ref, g_ref, dww_ref, dwb_ref, lng_ref, lnb_ref, pww_ref,
                     pwb_ref, a_ref, wa_ref, wc_ref, x_ref, o_ref, conv_even, conv_odd, hbuf, cbuf,
                     *, n_row_tiles, seq):
    i = pl.program_id(0)
    j = pl.program_id(1)
    rb = hc_ref.shape[0]
    tm = a_ref.shape[0]

    @pl.when((i == 0) & (j == 0))
    def _():
        conv_odd[...] = jnp.zeros(conv_odd.shape, conv_odd.dtype)

    pos = (jnp.minimum(i, n_row_tiles - 1) * tm + j * rb) % seq

    def phase(src, dst):
        acc = jnp.dot(a_ref[...], wa_ref[...], preferred_element_type=F32)
        acc = acc + jnp.dot(src[...], wc_ref[...], preferred_element_type=F32)
        o_ref[...] = x_ref[...] + acc
        _conv_load_rows(hp_ref, hc_ref, hn_ref, hbuf, first=pos == 0, last=pos + rb == seq)
        token = jnp.zeros((SUBLANES, LANES), jnp.uint32)
        for c in range(hc_ref.shape[1] // LANES):
            token = _conv_taps(hbuf, cbuf, dww_ref, dwb_ref, c, token)
        conv = _conv_finish(cbuf, g_ref, lng_ref, lnb_ref, pww_ref, pwb_ref)
        dst[pl.ds(pl.multiple_of(j * rb, rb), rb), :] = conv.astype(dst.dtype)

    @pl.when(i % 2 == 0)
    def _():
        phase(conv_odd, conv_even)

    @pl.when(i % 2 == 1)
    def _():
        phase(conv_even, conv_odd)


def _conv_out_proj(hglu, gates, attn, x2d, dw_w, dw_b, ln_g, ln_b, pw_w, pw_b, w_out, seq,
                   tm=1024, tn=512):
    t, d = x2d.shape
    width = hglu.shape[1]
    n_i, n_j = t // tm, d // tn
    rb = tm // n_j
    halo = BF16_SUBLANE_TILE
    assert attn.shape[1] == width and w_out.shape[0] == 2 * width and seq % rb == 0
    row = lambda v: v.reshape(1, width)
    once = dict(pipeline_mode=pl.Buffered(1))
    const = lambda i, j: (0, 0)

    def conv_row0(i, j):
        return jnp.minimum(i, n_i - 1) * tm + j * rb

    def out_tile(i, j):
        return (jnp.maximum(i - 1, 0), jnp.where(i == 0, 0, j))

    return pl.pallas_call(
        functools.partial(_conv_out_kernel, n_row_tiles=n_i, seq=seq),
        grid=(n_i + 1, n_j),
        in_specs=[pl.BlockSpec((halo, width),
                               lambda i, j: (jnp.maximum(conv_row0(i, j) // halo - 1, 0), 0)),
                  pl.BlockSpec((rb, width), lambda i, j: (conv_row0(i, j) // rb, 0)),
                  pl.BlockSpec((halo, width),
                               lambda i, j: (jnp.minimum((conv_row0(i, j) + rb) // halo,
                                                         t // halo - 1), 0)),
                  pl.BlockSpec((rb, width), lambda i, j: (conv_row0(i, j) // rb, 1)),
                  pl.BlockSpec((CONV_TAPS, width), const, **once),
                  pl.BlockSpec((1, width), const, **once),
                  pl.BlockSpec((1, width), const, **once),
                  pl.BlockSpec((1, width), const, **once),
                  pl.BlockSpec((width, width), const, **once),
                  pl.BlockSpec((1, width), const, **once),
                  pl.BlockSpec((tm, width), lambda i, j: (jnp.maximum(i - 1, 0), 0)),
                  pl.BlockSpec((width, tn), lambda i, j: (0, j)),
                  pl.BlockSpec((width, tn), lambda i, j: (1, j)),
                  pl.BlockSpec((tm, tn), out_tile)],
        out_specs=pl.BlockSpec((tm, tn), out_tile),
        out_shape=jax.ShapeDtypeStruct((t, d), F32),
        scratch_shapes=[pltpu.VMEM((tm, width), BF16),
                        pltpu.VMEM((tm, width), BF16),
                        pltpu.VMEM((rb + 2 * halo, width), F32),
                        pltpu.VMEM((rb, width), F32)],
        compiler_params=_params("arbitrary", "arbitrary"),
        name="conv_out_proj",
    )(hglu, hglu, hglu, gates, dw_w, row(dw_b), row(ln_g), row(ln_b), pw_w, row(pw_b),
      attn, w_out, w_out, x2d)


def _layer(x, norm_gain, w_in, q_gain, k_gain, sink, dw_w, dw_b, ln_g, ln_b, pw_w, pw_b, w_out):
    batch, seq, d = x.shape
    t = batch * seq
    qw = N_Q_HEADS * HEAD_DIM
    kvw = N_KV_HEADS * HEAD_DIM
    cw = dw_w.shape[1]
    x2d = x.reshape(t, d)

    c_v = qw + kvw
    c_ga = c_v + kvw
    c_ua = c_ga + qw
    c_ub = c_ua + cw
    c_gc = c_ub + cw
    w_bf = w_in.astype(BF16)

    cos_t, sa_t, sb_t = _rope_tables(seq)
    qg = (q_gain.astype(F32) * (LOG2E / math.sqrt(HEAD_DIM))).reshape(1, HEAD_DIM)
    kg = k_gain.astype(F32).reshape(1, HEAD_DIM)
    xn, kv = _norm_kv_proj(x2d, norm_gain, w_bf, qw // (2 * kvw), cos_t, sa_t, sb_t, kg, seq)

    tm = 1024
    rope_spec = pl.BlockSpec((tm, HEAD_DIM), lambda i, j: (i % (seq // tm), 0))
    gain_spec = pl.BlockSpec((1, HEAD_DIM), lambda i, j: (0, 0))
    tn_q = 1024
    q = _proj_call(_q_kernel, xn, w_bf, [lambda j: j], [cos_t, sa_t, sb_t, qg],
                   [rope_spec] * 3 + [gain_spec], tm, tn_q, qw // tn_q, "proj_q")
    tn_g = 1024
    n_ga = qw // tn_g
    gates = _proj_call(_silu_kernel, xn, w_bf,
                       [lambda j: jnp.where(j < n_ga, c_ga // tn_g + j, c_gc // tn_g + j - n_ga)],
                       [], [], tm, tn_g, (qw + cw) // tn_g, "proj_gates")
    tn_u = 512
    hglu = _proj_call(_glu_kernel, xn, w_bf,
                      [lambda j: c_ua // tn_u + j, lambda j: c_ub // tn_u + j],
                      [], [], tm, tn_u, cw // tn_u, "proj_glu")

    attn = _attention(q, kv, gates, sink.astype(F32), batch, seq)
    out = _conv_out_proj(hglu, gates, attn, x2d, dw_w, dw_b, ln_g, ln_b, pw_w.astype(BF16), pw_b,
                         w_out.astype(BF16), seq)
    return out.reshape(batch, seq, d)


def kernel(x, norm_gain, w_in, q_norm_gain, k_norm_gain, attn_sink, conv_dw_w, conv_dw_b,
           conv_ln_gain, conv_ln_bias, conv_pw_w, conv_pw_b, w_out):
    depth = norm_gain.shape[0]
    for layer in range(depth):
        x = _layer(x, norm_gain[layer], w_in[layer], q_norm_gain[layer], k_norm_gain[layer],
                   attn_sink[layer], conv_dw_w[layer], conv_dw_b[layer], conv_ln_gain[layer],
                   conv_ln_bias[layer], conv_pw_w[layer], conv_pw_b[layer], w_out[layer])
    return x
```

```python
import functools
import math

import jax
import jax.numpy as jnp
from jax import lax
from jax.experimental import pallas as pl
from jax.experimental.pallas import tpu as pltpu

F32 = jnp.float32
BF16 = jnp.bfloat16

HEAD_DIM = 128
N_Q_HEADS = 16
N_KV_HEADS = 4
GQA_GROUP = N_Q_HEADS // N_KV_HEADS
ATTN_BLOCK = 128
WINDOW = 128
ROPE_THETA = 500000.0
ROPE_DIM = HEAD_DIM // 4
ROPE_HALF = ROPE_DIM // 2
CONV_TAPS = 31
CONV_PAD = CONV_TAPS // 2
NORM_EPS = 1e-6
LN_EPS = 1e-5
MASK_VALUE = -1e30
LOG2E = math.log2(math.e)
EPILOGUE_ROWS = 256
NORM_ROWS = 64

LANES = 128
SUBLANES = 8
CONV_CHAIN_ROWS = 16
CONV_PARTIAL_SUMS = 2
BF16_SUBLANE_TILE = 16
VMEM_LIMIT_BYTES = 56 * 1024 * 1024


def _params(*sem):
    return pltpu.CompilerParams(dimension_semantics=sem, vmem_limit_bytes=VMEM_LIMIT_BYTES)


def _qk_tile(xn_ref, w_ref, cos_ref, sa_ref, sb_ref, gain_ref, o_ref, n_norm_heads):
    gain = gain_ref[...]
    rest = n_norm_heads * HEAD_DIM
    for r0 in range(0, xn_ref.shape[0], EPILOGUE_ROWS):
        rows = slice(r0, r0 + EPILOGUE_ROWS)
        acc = jnp.dot(xn_ref[rows, :], w_ref[...], preferred_element_type=F32)
        cos, sa, sb = cos_ref[rows, :], sa_ref[rows, :], sb_ref[rows, :]
        for h in range(n_norm_heads):
            xh = acc[:, h * HEAD_DIM:(h + 1) * HEAD_DIM]
            ms = jnp.mean(xh * xh, axis=-1, keepdims=True)
            y = xh * lax.rsqrt(ms + NORM_EPS) * gain
            y = (y * cos + pltpu.roll(y, HEAD_DIM - ROPE_HALF, 1) * sa
                 + pltpu.roll(y, ROPE_HALF, 1) * sb)
            o_ref[rows, h * HEAD_DIM:(h + 1) * HEAD_DIM] = y.astype(o_ref.dtype)
        if rest < acc.shape[1]:
            o_ref[rows, rest:] = acc[:, rest:].astype(o_ref.dtype)


def _q_kernel(xn_ref, w_ref, cos_ref, sa_ref, sb_ref, gain_ref, o_ref):
    _qk_tile(xn_ref, w_ref, cos_ref, sa_ref, sb_ref, gain_ref, o_ref, o_ref.shape[1] // HEAD_DIM)


def _norm_kv_kernel(x_ref, ng_ref, w_ref, cos_ref, sa_ref, sb_ref, gain_ref, xn_ref, kv_ref):
    ng = ng_ref[...]
    for r0 in range(0, x_ref.shape[0], NORM_ROWS):
        rows = slice(r0, r0 + NORM_ROWS)
        x = x_ref[rows, :]
        ms = jnp.mean(x * x, axis=-1, keepdims=True)
        xn_ref[rows, :] = (x * lax.rsqrt(ms + NORM_EPS) * ng).astype(xn_ref.dtype)
    _qk_tile(xn_ref, w_ref, cos_ref, sa_ref, sb_ref, gain_ref, kv_ref, N_KV_HEADS)


def _norm_kv_proj(x2d, norm_gain, w, w_col_block, cos_t, sa_t, sb_t, k_gain, seq, tm=512):
    t, d = x2d.shape
    tn = 2 * N_KV_HEADS * HEAD_DIM
    rope_spec = pl.BlockSpec((tm, HEAD_DIM), lambda i: (i % (seq // tm), 0))
    once = dict(pipeline_mode=pl.Buffered(1))
    return pl.pallas_call(
        _norm_kv_kernel,
        grid=(t // tm,),
        in_specs=[pl.BlockSpec((tm, d), lambda i: (i, 0)),
                  pl.BlockSpec((1, d), lambda i: (0, 0), **once),
                  pl.BlockSpec((d, tn), lambda i: (0, w_col_block), **once),
                  rope_spec, rope_spec, rope_spec,
                  pl.BlockSpec((1, HEAD_DIM), lambda i: (0, 0), **once)],
        out_specs=[pl.BlockSpec((tm, d), lambda i: (i, 0)),
                   pl.BlockSpec((tm, tn), lambda i: (i, 0))],
        out_shape=[jax.ShapeDtypeStruct((t, d), BF16),
                   jax.ShapeDtypeStruct((t, tn), BF16)],
        compiler_params=_params("parallel"),
        name="norm_proj_kv",
    )(x2d, norm_gain.reshape(1, d), w, cos_t, sa_t, sb_t, k_gain)


def _silu_kernel(xn_ref, w_ref, o_ref):
    for r0 in range(0, xn_ref.shape[0], EPILOGUE_ROWS):
        rows = slice(r0, r0 + EPILOGUE_ROWS)
        acc = jnp.dot(xn_ref[rows, :], w_ref[...], preferred_element_type=F32)
        o_ref[rows, :] = (acc * jax.nn.sigmoid(acc)).astype(o_ref.dtype)


def _glu_rows(xn_ref, wa_ref, wb_ref, o_ref, r0):
    rows = slice(r0, r0 + EPILOGUE_ROWS)
    xn = xn_ref[rows, :]
    a = jnp.dot(xn, wa_ref[...], preferred_element_type=F32)
    b = jnp.dot(xn, wb_ref[...], preferred_element_type=F32)
    o_ref[rows, :] = (a * jax.nn.sigmoid(b)).astype(o_ref.dtype)


def _proj_call(kernel, xn, w, w_col_blocks, extra_inputs, extra_specs, tm, tn, n_tiles, name):
    t, d = xn.shape
    w_specs = [pl.BlockSpec((d, tn), functools.partial(lambda i, j, f: (0, f(j)), f=f))
               for f in w_col_blocks]
    return pl.pallas_call(
        kernel,
        grid=(t // tm, n_tiles),
        in_specs=[pl.BlockSpec((tm, d), lambda i, j: (i, 0))] + w_specs + extra_specs,
        out_specs=pl.BlockSpec((tm, tn), lambda i, j: (i, j)),
        out_shape=jax.ShapeDtypeStruct((t, n_tiles * tn), BF16),
        compiler_params=_params("parallel", "arbitrary"),
        name=name,
    )(xn, *([w] * len(w_col_blocks)), *extra_inputs)


def _rope_tables(seq):
    inv_freq = jnp.power(jnp.float32(ROPE_THETA),
                         -jnp.arange(ROPE_HALF, dtype=F32) * (2.0 / ROPE_DIM))
    ang = jnp.arange(seq).astype(F32)[:, None] * inv_freq[None, :]
    cos, sin = jnp.cos(ang), jnp.sin(ang)
    zeros = jnp.zeros((seq, HEAD_DIM - ROPE_DIM), F32)
    zh = jnp.zeros((seq, ROPE_HALF), F32)
    cos_t = jnp.concatenate([cos, cos, jnp.ones_like(zeros)], axis=-1)
    sa_t = jnp.concatenate([-sin, zh, zeros], axis=-1)
    sb_t = jnp.concatenate([zh, sin, zeros], axis=-1)
    return cos_t, sa_t, sb_t


def _attn_scores(q_ref, kv, bias, u, h):
    blk = ATTN_BLOCK
    rows = slice(u * blk, (u + 1) * blk)
    qs = jnp.concatenate(
        [q_ref[rows, (h * GQA_GROUP + g) * HEAD_DIM:(h * GQA_GROUP + g + 1) * HEAD_DIM]
         for g in range(GQA_GROUP)], axis=0)
    kh = kv[u * blk:(u + 3) * blk, h * HEAD_DIM:(h + 1) * HEAD_DIM]
    return lax.dot_general(kh, qs, (((1,), (1,)), ((), ())),
                           preferred_element_type=F32) + bias


def _attn_finish(sink_ref, g_ref, o_ref, kv, s, u, h):
    blk = ATTN_BLOCK
    kvw = N_KV_HEADS * HEAD_DIM
    rows = slice(u * blk, (u + 1) * blk)
    vh = kv[u * blk:(u + 3) * blk, kvw + h * HEAD_DIM:kvw + (h + 1) * HEAD_DIM]
    lane_head = lax.broadcasted_iota(jnp.int32, (1, GQA_GROUP * blk), 1) // blk
    sink = jnp.zeros((1, GQA_GROUP * blk), F32)
    for g in range(GQA_GROUP):
        sink = jnp.where(lane_head == g, sink_ref[h * GQA_GROUP + g] * LOG2E, sink)
    m = jnp.maximum(jnp.max(s, axis=0, keepdims=True), sink)
    p = jnp.exp2(s - m)
    den = jnp.sum(p, axis=0, keepdims=True) + jnp.exp2(sink - m)
    ot = lax.dot_general(vh, p.astype(BF16), (((0,), (0,)), ((), ())),
                         preferred_element_type=F32)
    ot = ot / den
    for g in range(GQA_GROUP):
        c0 = (h * GQA_GROUP + g) * HEAD_DIM
        gate = g_ref[rows, c0:c0 + HEAD_DIM].astype(F32)
        o = ot[:, g * blk:(g + 1) * blk].T
        o_ref[rows, c0:c0 + HEAD_DIM] = (o * gate).astype(o_ref.dtype)


def _glu_attn_kernel(sink_ref, xn_ref, wa_ref, wb_ref, q_ref, kvp_ref, kvc_ref, kvn_ref, g_ref,
                     h_ref, o_ref, *, blocks_per_seq):
    i = pl.program_id(0)
    j = pl.program_id(1)
    blk = ATTN_BLOCK
    n_blocks = q_ref.shape[0] // blk
    block0 = (i * pl.num_programs(1) + j) * n_blocks
    first = block0 % blocks_per_seq == 0
    last = (block0 + n_blocks) % blocks_per_seq == 0
    kv = jnp.concatenate([kvp_ref[...], kvc_ref[...], kvn_ref[...]], axis=0)
    key = lax.broadcasted_iota(jnp.int32, (3 * blk, blk), 0)
    qry = lax.broadcasted_iota(jnp.int32, (3 * blk, blk), 1)
    in_band = jnp.abs(key - blk - qry) <= WINDOW
    biases = []
    for u in range(n_blocks):
        valid = in_band
        if u == 0:
            valid = valid & ((key >= blk) | jnp.logical_not(first))
        if u == n_blocks - 1:
            valid = valid & ((key < 2 * blk) | jnp.logical_not(last))
        bias1 = jnp.where(valid, 0.0, MASK_VALUE).astype(F32)
        biases.append(jnp.concatenate([bias1] * GQA_GROUP, axis=1))

    units = [(u, h) for u in range(n_blocks) for h in range(N_KV_HEADS)]
    chunks = list(range(0, xn_ref.shape[0], EPILOGUE_ROWS))
    per = len(units) // len(chunks)
    scores = {}
    for u, h in units[:per]:
        scores[(u, h)] = _attn_scores(q_ref, kv, biases[u], u, h)
    for k, r0 in enumerate(chunks):
        for u, h in units[(k + 1) * per:(k + 2) * per]:
            scores[(u, h)] = _attn_scores(q_ref, kv, biases[u], u, h)
        _glu_rows(xn_ref, wa_ref, wb_ref, h_ref, r0)
        for u, h in units[k * per:(k + 1) * per]:
            _attn_finish(sink_ref, g_ref, o_ref, kv, scores.pop((u, h)), u, h)


def _glu_attention(xn, w, wa_col0, wb_col0, q, kv, gates, sink, seq, tm=1024, tn=512):
    t, d = xn.shape
    qw = q.shape[1]
    kv2 = kv.shape[1]
    blk = ATTN_BLOCK
    n_i = t // tm
    n_j = (wb_col0 - wa_col0) // tn
    rows = tm // n_j
    u = rows // blk
    assert rows % blk == 0 and seq % rows == 0
    assert (u * N_KV_HEADS) % (tm // EPILOGUE_ROWS) == 0
    nblk = t // blk
    step = lambda i, j: i * n_j + j
    return pl.pallas_call(
        functools.partial(_glu_attn_kernel, blocks_per_seq=seq // blk),
        grid=(n_i, n_j),
        in_specs=[pl.BlockSpec(memory_space=pltpu.SMEM),
                  pl.BlockSpec((tm, d), lambda i, j: (i, 0)),
                  pl.BlockSpec((d, tn), lambda i, j: (0, wa_col0 // tn + j)),
                  pl.BlockSpec((d, tn), lambda i, j: (0, wb_col0 // tn + j)),
                  pl.BlockSpec((rows, qw), lambda i, j: (step(i, j), 0)),
                  pl.BlockSpec((blk, kv2), lambda i, j: (jnp.maximum(step(i, j) * u - 1, 0), 0)),
                  pl.BlockSpec((rows, kv2), lambda i, j: (step(i, j), 0)),
                  pl.BlockSpec((blk, kv2),
                               lambda i, j: (jnp.minimum(step(i, j) * u + u, nblk - 1), 0)),
                  pl.BlockSpec((rows, qw), lambda i, j: (step(i, j), 0))],
        out_specs=[pl.BlockSpec((tm, tn), lambda i, j: (i, j)),
                   pl.BlockSpec((rows, qw), lambda i, j: (step(i, j), 0))],
        out_shape=[jax.ShapeDtypeStruct((t, n_j * tn), BF16),
                   jax.ShapeDtypeStruct((t, qw), BF16)],
        compiler_params=_params("parallel", "arbitrary"),
        name="proj_glu_attention",
    )(sink, xn, w, w, q, kv, kv, kv, gates)


def _conv_load_rows(hp_ref, hc_ref, hn_ref, hbuf, first, last):
    rb = hc_ref.shape[0]
    halo = BF16_SUBLANE_TILE
    hbuf[0:halo, :] = jnp.where(first, 0.0, hp_ref[...].astype(F32))
    hbuf[halo:halo + rb, :] = hc_ref[...].astype(F32)
    hbuf[halo + rb:, :] = jnp.where(last, 0.0, hn_ref[...].astype(F32))


def _zero_bits_after(v):
    u = lax.bitcast_convert_type(v[:SUBLANES], jnp.uint32)
    return lax.shift_right_logical(lax.shift_right_logical(u, jnp.uint32(16)), jnp.uint32(16))


def _conv_taps(hbuf, cbuf, dww_ref, dwb_ref, c, token):
    rb = cbuf.shape[0]
    base = BF16_SUBLANE_TILE - CONV_PAD
    span = rb + ((base + CONV_TAPS - 1) // SUBLANES) * SUBLANES
    cs = slice(c * LANES, (c + 1) * LANES)
    rows = hbuf[0:span + SUBLANES, cs]
    shifted = [rows] + [pltpu.roll(rows, span + SUBLANES - b, 0) for b in range(1, SUBLANES)]
    gr = CONV_CHAIN_ROWS
    bias = lax.bitcast_convert_type(jnp.broadcast_to(dwb_ref[:, cs], (gr, LANES)), jnp.uint32)
    for g0 in range(0, rb, gr):
        acc = lax.bitcast_convert_type(
            bias | jnp.concatenate([token] * (gr // SUBLANES), axis=0), F32)
        partial = [acc] + [None] * (CONV_PARTIAL_SUMS - 1)
        for t in range(CONV_TAPS):
            b = (base + t) % SUBLANES
            a8 = base + t - b + g0
            term = shifted[b][a8:a8 + gr] * dww_ref[t:t + 1, cs]
            k = t % CONV_PARTIAL_SUMS
            partial[k] = term if partial[k] is None else partial[k] + term
        acc = functools.reduce(lambda u, v: u + v, partial)
        cbuf[g0:g0 + gr, cs] = acc
        token = _zero_bits_after(acc)
    return token


def _conv_finish(cbuf, g_ref, lng_ref, lnb_ref, pww_ref, pwb_ref):
    y = cbuf[...]
    mu = jnp.mean(y, axis=-1, keepdims=True)
    yc = y - mu
    var = jnp.mean(yc * yc, axis=-1, keepdims=True)
    yn = yc * lax.rsqrt(var + LN_EPS) * lng_ref[...] + lnb_ref[...]
    z = (yn * jax.nn.sigmoid(yn)).astype(BF16)
    o = jnp.dot(z, pww_ref[...], preferred_element_type=F32) + pwb_ref[...]
    return o * g_ref[...].astype(F32)


def _conv_out_kernel(hp_ref, hc_ref, hn_ref, g_ref, dww_ref, dwb_ref, lng_ref, lnb_ref, pww_ref,
                     pwb_ref, a_ref, wa_ref, wc_ref, x_ref, o_ref, conv_even, conv_odd, hbuf, cbuf,
                     *, n_row_tiles, seq):
    i = pl.program_id(0)
    j = pl.program_id(1)
    rb = hc_ref.shape[0]
    tm = a_ref.shape[0]

    @pl.when((i == 0) & (j == 0))
    def _():
        conv_odd[...] = jnp.zeros(conv_odd.shape, conv_odd.dtype)

    pos = (jnp.minimum(i, n_row_tiles - 1) * tm + j * rb) % seq

    def phase(src, dst):
        acc = jnp.dot(a_ref[...], wa_ref[...], preferred_element_type=F32)
        acc = acc + jnp.dot(src[...], wc_ref[...], preferred_element_type=F32)
        o_ref[...] = x_ref[...] + acc
        _conv_load_rows(hp_ref, hc_ref, hn_ref, hbuf, first=pos == 0, last=pos + rb == seq)
        token = jnp.zeros((SUBLANES, LANES), jnp.uint32)
        for c in range(hc_ref.shape[1] // LANES):
            token = _conv_taps(hbuf, cbuf, dww_ref, dwb_ref, c, token)
        conv = _conv_finish(cbuf, g_ref, lng_ref, lnb_ref, pww_ref, pwb_ref)
        dst[pl.ds(pl.multiple_of(j * rb, rb), rb), :] = conv.astype(dst.dtype)

    @pl.when(i % 2 == 0)
    def _():
        phase(conv_odd, conv_even)

    @pl.when(i % 2 == 1)
    def _():
        phase(conv_even, conv_odd)


def _conv_out_proj(hglu, gates, attn, x2d, dw_w, dw_b, ln_g, ln_b, pw_w, pw_b, w_out, seq,
                   tm=1024, tn=512):
    t, d = x2d.shape
    width = hglu.shape[1]
    n_i, n_j = t // tm, d // tn
    rb = tm // n_j
    halo = BF16_SUBLANE_TILE
    assert attn.shape[1] == width and w_out.shape[0] == 2 * width and seq % rb == 0
    row = lambda v: v.reshape(1, width)
    once = dict(pipeline_mode=pl.Buffered(1))
    const = lambda i, j: (0, 0)

    def conv_row0(i, j):
        return jnp.minimum(i, n_i - 1) * tm + j * rb

    def out_tile(i, j):
        return (jnp.maximum(i - 1, 0), jnp.where(i == 0, 0, j))

    return pl.pallas_call(
        functools.partial(_conv_out_kernel, n_row_tiles=n_i, seq=seq),
        grid=(n_i + 1, n_j),
        in_specs=[pl.BlockSpec((halo, width),
                               lambda i, j: (jnp.maximum(conv_row0(i, j) // halo - 1, 0), 0)),
                  pl.BlockSpec((rb, width), lambda i, j: (conv_row0(i, j) // rb, 0)),
                  pl.BlockSpec((halo, width),
                               lambda i, j: (jnp.minimum((conv_row0(i, j) + rb) // halo,
                                                         t // halo - 1), 0)),
                  pl.BlockSpec((rb, width), lambda i, j: (conv_row0(i, j) // rb, 1)),
                  pl.BlockSpec((CONV_TAPS, width), const, **once),
                  pl.BlockSpec((1, width), const, **once),
                  pl.BlockSpec((1, width), const, **once),
                  pl.BlockSpec((1, width), const, **once),
                  pl.BlockSpec((width, width), const, **once),
                  pl.BlockSpec((1, width), const, **once),
                  pl.BlockSpec((tm, width), lambda i, j: (jnp.maximum(i - 1, 0), 0)),
                  pl.BlockSpec((width, tn), lambda i, j: (0, j)),
                  pl.BlockSpec((width, tn), lambda i, j: (1, j)),
                  pl.BlockSpec((tm, tn), out_tile)],
        out_specs=pl.BlockSpec((tm, tn), out_tile),
        out_shape=jax.ShapeDtypeStruct((t, d), F32),
        scratch_shapes=[pltpu.VMEM((tm, width), BF16),
                        pltpu.VMEM((tm, width), BF16),
                        pltpu.VMEM((rb + 2 * halo, width), F32),
                        pltpu.VMEM((rb, width), F32)],
        compiler_params=_params("arbitrary", "arbitrary"),
        name="conv_out_proj",
    )(hglu, hglu, hglu, gates, dw_w, row(dw_b), row(ln_g), row(ln_b), pw_w, row(pw_b),
      attn, w_out, w_out, x2d)


def _layer(x, norm_gain, w_in, q_gain, k_gain, sink, dw_w, dw_b, ln_g, ln_b, pw_w, pw_b, w_out):
    batch, seq, d = x.shape
    t = batch * seq
    qw = N_Q_HEADS * HEAD_DIM
    kvw = N_KV_HEADS * HEAD_DIM
    cw = dw_w.shape[1]
    x2d = x.reshape(t, d)

    c_v = qw + kvw
    c_ga = c_v + kvw
    c_ua = c_ga + qw
    c_ub = c_ua + cw
    c_gc = c_ub + cw
    w_bf = w_in.astype(BF16)

    cos_t, sa_t, sb_t = _rope_tables(seq)
    qg = (q_gain.astype(F32) * (LOG2E / math.sqrt(HEAD_DIM))).reshape(1, HEAD_DIM)
    kg = k_gain.astype(F32).reshape(1, HEAD_DIM)
    xn, kv = _norm_kv_proj(x2d, norm_gain, w_bf, qw // (2 * kvw), cos_t, sa_t, sb_t, kg, seq)

    tm = 1024
    rope_spec = pl.BlockSpec((tm, HEAD_DIM), lambda i, j: (i % (seq // tm), 0))
    gain_spec = pl.BlockSpec((1, HEAD_DIM), lambda i, j: (0, 0))
    tn_q = 1024
    q = _proj_call(_q_kernel, xn, w_bf, [lambda j: j], [cos_t, sa_t, sb_t, qg],
                   [rope_spec] * 3 + [gain_spec], tm, tn_q, qw // tn_q, "proj_q")
    tn_g = 1024
    n_ga = qw // tn_g
    gates = _proj_call(_silu_kernel, xn, w_bf,
                       [lambda j: jnp.where(j < n_ga, c_ga // tn_g + j, c_gc // tn_g + j - n_ga)],
                       [], [], tm, tn_g, (qw + cw) // tn_g, "proj_gates")
    hglu, attn = _glu_attention(xn, w_bf, c_ua, c_ub, q, kv, gates, sink.astype(F32), seq)
    out = _conv_out_proj(hglu, gates, attn, x2d, dw_w, dw_b, ln_g, ln_b, pw_w.astype(BF16), pw_b,
                         w_out.astype(BF16), seq)
    return out.reshape(batch, seq, d)


def kernel(x, norm_gain, w_in, q_norm_gain, k_norm_gain, attn_sink, conv_dw_w, conv_dw_b,
           conv_ln_gain, conv_ln_bias, conv_pw_w, conv_pw_b, w_out):
    depth = norm_gain.shape[0]
    for layer in range(depth):
        x = _layer(x, norm_gain[layer], w_in[layer], q_norm_gain[layer], k_norm_gain[layer],
                   attn_sink[layer], conv_dw_w[layer], conv_dw_b[layer], conv_ln_gain[layer],
                   conv_ln_bias[layer], conv_pw_w[layer], conv_pw_b[layer], w_out[layer])
    return x
```

```python
import functools
import math

import jax
import jax.numpy as jnp
from jax import lax
from jax.experimental import pallas as pl
from jax.experimental.pallas import tpu as pltpu

F32 = jnp.float32
BF16 = jnp.bfloat16

HEAD_DIM = 128
N_Q_HEADS = 16
N_KV_HEADS = 4
GQA_GROUP = N_Q_HEADS // N_KV_HEADS
ATTN_BLOCK = 128
WINDOW = 128
ROPE_THETA = 500000.0
ROPE_DIM = HEAD_DIM // 4
ROPE_HALF = ROPE_DIM // 2
CONV_TAPS = 31
CONV_PAD = CONV_TAPS // 2
NORM_EPS = 1e-6
LN_EPS = 1e-5
MASK_VALUE = -1e30
LOG2E = math.log2(math.e)
EPILOGUE_ROWS = 256
NORM_ROWS = 64

LANES = 128
SUBLANES = 8
CONV_CHAIN_ROWS = 16
CONV_PARTIAL_SUMS = 2
BF16_SUBLANE_TILE = 16
VMEM_LIMIT_BYTES = 56 * 1024 * 1024


def _params(*sem):
    return pltpu.CompilerParams(dimension_semantics=sem, vmem_limit_bytes=VMEM_LIMIT_BYTES)


def _qk_tile(xn_ref, w_ref, cos_ref, sa_ref, sb_ref, gain_ref, o_ref, n_norm_heads):
    gain = gain_ref[...]
    rest = n_norm_heads * HEAD_DIM
    for r0 in range(0, xn_ref.shape[0], EPILOGUE_ROWS):
        rows = slice(r0, r0 + EPILOGUE_ROWS)
        acc = jnp.dot(xn_ref[rows, :], w_ref[...], preferred_element_type=F32)
        cos, sa, sb = cos_ref[rows, :], sa_ref[rows, :], sb_ref[rows, :]
        for h in range(n_norm_heads):
            xh = acc[:, h * HEAD_DIM:(h + 1) * HEAD_DIM]
            ms = jnp.mean(xh * xh, axis=-1, keepdims=True)
            y = xh * lax.rsqrt(ms + NORM_EPS) * gain
            y = (y * cos + pltpu.roll(y, HEAD_DIM - ROPE_HALF, 1) * sa
                 + pltpu.roll(y, ROPE_HALF, 1) * sb)
            o_ref[rows, h * HEAD_DIM:(h + 1) * HEAD_DIM] = y.astype(o_ref.dtype)
        if rest < acc.shape[1]:
            o_ref[rows, rest:] = acc[:, rest:].astype(o_ref.dtype)


def _q_kernel(xn_ref, w_ref, cos_ref, sa_ref, sb_ref, gain_ref, o_ref):
    _qk_tile(xn_ref, w_ref, cos_ref, sa_ref, sb_ref, gain_ref, o_ref, o_ref.shape[1] // HEAD_DIM)


def _norm_kv_kernel(x_ref, ng_ref, w_ref, cos_ref, sa_ref, sb_ref, gain_ref, xn_ref, kv_ref):
    ng = ng_ref[...]
    for r0 in range(0, x_ref.shape[0], NORM_ROWS):
        rows = slice(r0, r0 + NORM_ROWS)
        x = x_ref[rows, :]
        ms = jnp.mean(x * x, axis=-1, keepdims=True)
        xn_ref[rows, :] = (x * lax.rsqrt(ms + NORM_EPS) * ng).astype(xn_ref.dtype)
    _qk_tile(xn_ref, w_ref, cos_ref, sa_ref, sb_ref, gain_ref, kv_ref, N_KV_HEADS)


def _norm_kv_proj(x2d, norm_gain, w, w_col_block, cos_t, sa_t, sb_t, k_gain, seq, tm=512):
    t, d = x2d.shape
    tn = 2 * N_KV_HEADS * HEAD_DIM
    rope_spec = pl.BlockSpec((tm, HEAD_DIM), lambda i: (i % (seq // tm), 0))
    once = dict(pipeline_mode=pl.Buffered(1))
    return pl.pallas_call(
        _norm_kv_kernel,
        grid=(t // tm,),
        in_specs=[pl.BlockSpec((tm, d), lambda i: (i, 0)),
                  pl.BlockSpec((1, d), lambda i: (0, 0), **once),
                  pl.BlockSpec((d, tn), lambda i: (0, w_col_block), **once),
                  rope_spec, rope_spec, rope_spec,
                  pl.BlockSpec((1, HEAD_DIM), lambda i: (0, 0), **once)],
        out_specs=[pl.BlockSpec((tm, d), lambda i: (i, 0)),
                   pl.BlockSpec((tm, tn), lambda i: (i, 0))],
        out_shape=[jax.ShapeDtypeStruct((t, d), BF16),
                   jax.ShapeDtypeStruct((t, tn), BF16)],
        compiler_params=_params("parallel"),
        name="norm_proj_kv",
    )(x2d, norm_gain.reshape(1, d), w, cos_t, sa_t, sb_t, k_gain)


def _silu_kernel(xn_ref, w_ref, o_ref):
    for r0 in range(0, xn_ref.shape[0], EPILOGUE_ROWS):
        rows = slice(r0, r0 + EPILOGUE_ROWS)
        acc = jnp.dot(xn_ref[rows, :], w_ref[...], preferred_element_type=F32)
        o_ref[rows, :] = (acc * jax.nn.sigmoid(acc)).astype(o_ref.dtype)


def _glu_rows(xn_ref, wa_ref, wb_ref, o_ref, r0):
    rows = slice(r0, r0 + EPILOGUE_ROWS)
    xn = xn_ref[rows, :]
    a = jnp.dot(xn, wa_ref[...], preferred_element_type=F32)
    b = jnp.dot(xn, wb_ref[...], preferred_element_type=F32)
    o_ref[rows, :] = (a * jax.nn.sigmoid(b)).astype(o_ref.dtype)


def _proj_call(kernel, xn, w, w_col_blocks, extra_inputs, extra_specs, tm, tn, n_tiles, name):
    t, d = xn.shape
    w_specs = [pl.BlockSpec((d, tn), functools.partial(lambda i, j, f: (0, f(j)), f=f))
               for f in w_col_blocks]
    return pl.pallas_call(
        kernel,
        grid=(t // tm, n_tiles),
        in_specs=[pl.BlockSpec((tm, d), lambda i, j: (i, 0))] + w_specs + extra_specs,
        out_specs=pl.BlockSpec((tm, tn), lambda i, j: (i, j)),
        out_shape=jax.ShapeDtypeStruct((t, n_tiles * tn), BF16),
        compiler_params=_params("parallel", "arbitrary"),
        name=name,
    )(xn, *([w] * len(w_col_blocks)), *extra_inputs)


def _rope_tables(seq):
    inv_freq = jnp.power(jnp.float32(ROPE_THETA),
                         -jnp.arange(ROPE_HALF, dtype=F32) * (2.0 / ROPE_DIM))
    ang = jnp.arange(seq).astype(F32)[:, None] * inv_freq[None, :]
    cos, sin = jnp.cos(ang), jnp.sin(ang)
    zeros = jnp.zeros((seq, HEAD_DIM - ROPE_DIM), F32)
    zh = jnp.zeros((seq, ROPE_HALF), F32)
    cos_t = jnp.concatenate([cos, cos, jnp.ones_like(zeros)], axis=-1)
    sa_t = jnp.concatenate([-sin, zh, zeros], axis=-1)
    sb_t = jnp.concatenate([zh, sin, zeros], axis=-1)
    return cos_t, sa_t, sb_t


def _attn_scores(q_ref, kv, bias, u, h):
    blk = ATTN_BLOCK
    rows = slice(u * blk, (u + 1) * blk)
    qs = jnp.concatenate(
        [q_ref[rows, (h * GQA_GROUP + g) * HEAD_DIM:(h * GQA_GROUP + g + 1) * HEAD_DIM]
         for g in range(GQA_GROUP)], axis=0)
    kh = kv[u * blk:(u + 3) * blk, h * HEAD_DIM:(h + 1) * HEAD_DIM]
    return lax.dot_general(kh, qs, (((1,), (1,)), ((), ())),
                           preferred_element_type=F32) + bias


def _attn_finish(sink_ref, g_ref, o_ref, kv, s, u, h):
    blk = ATTN_BLOCK
    kvw = N_KV_HEADS * HEAD_DIM
    rows = slice(u * blk, (u + 1) * blk)
    vh = kv[u * blk:(u + 3) * blk, kvw + h * HEAD_DIM:kvw + (h + 1) * HEAD_DIM]
    lane_head = lax.broadcasted_iota(jnp.int32, (1, GQA_GROUP * blk), 1) // blk
    sink = jnp.zeros((1, GQA_GROUP * blk), F32)
    for g in range(GQA_GROUP):
        sink = jnp.where(lane_head == g, sink_ref[h * GQA_GROUP + g] * LOG2E, sink)
    m = jnp.maximum(jnp.max(s, axis=0, keepdims=True), sink)
    p = jnp.exp2(s - m)
    den = jnp.sum(p, axis=0, keepdims=True) + jnp.exp2(sink - m)
    ot = lax.dot_general(vh, p.astype(BF16), (((0,), (0,)), ((), ())),
                         preferred_element_type=F32)
    ot = ot / den
    for g in range(GQA_GROUP):
        c0 = (h * GQA_GROUP + g) * HEAD_DIM
        gate = g_ref[rows, c0:c0 + HEAD_DIM].astype(F32)
        o = ot[:, g * blk:(g + 1) * blk].T
        o_ref[rows, c0:c0 + HEAD_DIM] = (o * gate).astype(o_ref.dtype)


def _glu_attn_kernel(sink_ref, xn_ref, wa_ref, wb_ref, q_ref, kvp_ref, kvc_ref, kvn_ref, g_ref,
                     h_ref, o_ref, *, blocks_per_seq):
    i = pl.program_id(0)
    j = pl.program_id(1)
    blk = ATTN_BLOCK
    n_blocks = q_ref.shape[0] // blk
    block0 = (i * pl.num_programs(1) + j) * n_blocks
    first = block0 % blocks_per_seq == 0
    last = (block0 + n_blocks) % blocks_per_seq == 0
    kv = jnp.concatenate([kvp_ref[...], kvc_ref[...], kvn_ref[...]], axis=0)
    key = lax.broadcasted_iota(jnp.int32, (3 * blk, blk), 0)
    qry = lax.broadcasted_iota(jnp.int32, (3 * blk, blk), 1)
    in_band = jnp.abs(key - blk - qry) <= WINDOW
    biases = []
    for u in range(n_blocks):
        valid = in_band
        if u == 0:
            valid = valid & ((key >= blk) | jnp.logical_not(first))
        if u == n_blocks - 1:
            valid = valid & ((key < 2 * blk) | jnp.logical_not(last))
        bias1 = jnp.where(valid, 0.0, MASK_VALUE).astype(F32)
        biases.append(jnp.concatenate([bias1] * GQA_GROUP, axis=1))

    units = [(u, h) for u in range(n_blocks) for h in range(N_KV_HEADS)]
    chunks = list(range(0, xn_ref.shape[0], EPILOGUE_ROWS))
    per = len(units) // len(chunks)
    scores = {}
    for u, h in units[:per]:
        scores[(u, h)] = _attn_scores(q_ref, kv, biases[u], u, h)
    for k, r0 in enumerate(chunks):
        for u, h in units[(k + 1) * per:(k + 2) * per]:
            scores[(u, h)] = _attn_scores(q_ref, kv, biases[u], u, h)
        _glu_rows(xn_ref, wa_ref, wb_ref, h_ref, r0)
        for u, h in units[k * per:(k + 1) * per]:
            _attn_finish(sink_ref, g_ref, o_ref, kv, scores.pop((u, h)), u, h)


def _glu_attention(xn, w, wa_col0, wb_col0, q, kv, gates, sink, seq, tm=1024, tn=512):
    t, d = xn.shape
    qw = q.shape[1]
    kv2 = kv.shape[1]
    blk = ATTN_BLOCK
    n_i = t // tm
    n_j = (wb_col0 - wa_col0) // tn
    rows = tm // n_j
    u = rows // blk
    assert rows % blk == 0 and seq % rows == 0
    assert (u * N_KV_HEADS) % (tm // EPILOGUE_ROWS) == 0
    nblk = t // blk
    step = lambda i, j: i * n_j + j
    return pl.pallas_call(
        functools.partial(_glu_attn_kernel, blocks_per_seq=seq // blk),
        grid=(n_i, n_j),
        in_specs=[pl.BlockSpec(memory_space=pltpu.SMEM),
                  pl.BlockSpec((tm, d), lambda i, j: (i, 0)),
                  pl.BlockSpec((d, tn), lambda i, j: (0, wa_col0 // tn + j)),
                  pl.BlockSpec((d, tn), lambda i, j: (0, wb_col0 // tn + j)),
                  pl.BlockSpec((rows, qw), lambda i, j: (step(i, j), 0)),
                  pl.BlockSpec((blk, kv2), lambda i, j: (jnp.maximum(step(i, j) * u - 1, 0), 0)),
                  pl.BlockSpec((rows, kv2), lambda i, j: (step(i, j), 0)),
                  pl.BlockSpec((blk, kv2),
                               lambda i, j: (jnp.minimum(step(i, j) * u + u, nblk - 1), 0)),
                  pl.BlockSpec((rows, qw), lambda i, j: (step(i, j), 0))],
        out_specs=[pl.BlockSpec((tm, tn), lambda i, j: (i, j)),
                   pl.BlockSpec((rows, qw), lambda i, j: (step(i, j), 0))],
        out_shape=[jax.ShapeDtypeStruct((t, n_j * tn), BF16),
                   jax.ShapeDtypeStruct((t, qw), BF16)],
        compiler_params=_params("parallel", "arbitrary"),
        name="proj_glu_attention",
    )(sink, xn, w, w, q, kv, kv, kv, gates)


def _conv_load_rows(hp_ref, hc_ref, hn_ref, hbuf, first, last):
    rb = hc_ref.shape[0]
    halo = BF16_SUBLANE_TILE
    hbuf[0:halo, :] = jnp.where(first, 0.0, hp_ref[...].astype(F32))
    hbuf[halo:halo + rb, :] = hc_ref[...].astype(F32)
    hbuf[halo + rb:, :] = jnp.where(last, 0.0, hn_ref[...].astype(F32))


def _zero_bits_after(v):
    u = lax.bitcast_convert_type(v[:SUBLANES], jnp.uint32)
    return lax.shift_right_logical(lax.shift_right_logical(u, jnp.uint32(16)), jnp.uint32(16))


def _conv_taps(hbuf, cbuf, dww_ref, dwb_ref, c, token):
    rb = cbuf.shape[0]
    base = BF16_SUBLANE_TILE - CONV_PAD
    span = rb + ((base + CONV_TAPS - 1) // SUBLANES) * SUBLANES
    cs = slice(c * LANES, (c + 1) * LANES)
    rows = hbuf[0:span + SUBLANES, cs]
    shifted = [rows] + [pltpu.roll(rows, span + SUBLANES - b, 0) for b in range(1, SUBLANES)]
    gr = CONV_CHAIN_ROWS
    bias = lax.bitcast_convert_type(jnp.broadcast_to(dwb_ref[:, cs], (gr, LANES)), jnp.uint32)
    for g0 in range(0, rb, gr):
        acc = lax.bitcast_convert_type(
            bias | jnp.concatenate([token] * (gr // SUBLANES), axis=0), F32)
        partial = [acc] + [None] * (CONV_PARTIAL_SUMS - 1)
        for t in range(CONV_TAPS):
            b = (base + t) % SUBLANES
            a8 = base + t - b + g0
            term = shifted[b][a8:a8 + gr] * dww_ref[t:t + 1, cs]
            k = t % CONV_PARTIAL_SUMS
            partial[k] = term if partial[k] is None else partial[k] + term
        acc = functools.reduce(lambda u, v: u + v, partial)
        cbuf[g0:g0 + gr, cs] = acc
        token = _zero_bits_after(acc)
    return token


def _conv_finish(cbuf, g_ref, lng_ref, lnb_ref, pww_ref, pwb_ref):
    y = cbuf[...]
    mu = jnp.mean(y, axis=-1, keepdims=True)
    yc = y - mu
    var = jnp.mean(yc * yc, axis=-1, keepdims=True)
    yn = yc * lax.rsqrt(var + LN_EPS) * lng_ref[...] + lnb_ref[...]
    z = (yn * jax.nn.sigmoid(yn)).astype(BF16)
    o = jnp.dot(z, pww_ref[...], preferred_element_type=F32) + pwb_ref[...]
    return o * g_ref[...].astype(F32)


def _conv_out_kernel(hp_ref, hc_ref, hn_ref, g_ref, dww_ref, dwb_ref, lng_ref, lnb_ref, pww_ref,
                     pwb_ref, a_ref, wa_ref, wc_ref, x_ref, o_ref, conv_even, conv_odd, hbuf, cbuf,
                     *, n_row_tiles, seq):
    i = pl.program_id(0)
    j = pl.program_id(1)
    rb = hc_ref.shape[0]
    tm = a_ref.shape[0]
    pos = (jnp.minimum(i, n_row_tiles - 1) * tm + j * rb) % seq

    def phase(src, dst):
        if src is not None:
            acc = jnp.dot(a_ref[...], wa_ref[...], preferred_element_type=F32)
            acc = acc + jnp.dot(src[...], wc_ref[...], preferred_element_type=F32)
            o_ref[...] = x_ref[...] + acc
        if dst is not None:
            _conv_load_rows(hp_ref, hc_ref, hn_ref, hbuf, first=pos == 0, last=pos + rb == seq)
            token = jnp.zeros((SUBLANES, LANES), jnp.uint32)
            for c in range(hc_ref.shape[1] // LANES):
                token = _conv_taps(hbuf, cbuf, dww_ref, dwb_ref, c, token)
            conv = _conv_finish(cbuf, g_ref, lng_ref, lnb_ref, pww_ref, pwb_ref)
            dst[pl.ds(pl.multiple_of(j * rb, rb), rb), :] = conv.astype(dst.dtype)

    first_phase = i == 0
    last_phase = i == n_row_tiles
    steady = jnp.logical_not(first_phase | last_phase)
    last_src = conv_odd if n_row_tiles % 2 == 0 else conv_even

    @pl.when(first_phase)
    def _():
        phase(None, conv_even)

    @pl.when(steady & (i % 2 == 0))
    def _():
        phase(conv_odd, conv_even)

    @pl.when(steady & (i % 2 == 1))
    def _():
        phase(conv_even, conv_odd)

    @pl.when(last_phase)
    def _():
        phase(last_src, None)


def _conv_out_proj(hglu, gates, attn, x2d, dw_w, dw_b, ln_g, ln_b, pw_w, pw_b, w_out, seq,
                   tm=1024, tn=512):
    t, d = x2d.shape
    width = hglu.shape[1]
    n_i, n_j = t // tm, d // tn
    rb = tm // n_j
    halo = BF16_SUBLANE_TILE
    assert attn.shape[1] == width and w_out.shape[0] == 2 * width and seq % rb == 0
    row = lambda v: v.reshape(1, width)
    once = dict(pipeline_mode=pl.Buffered(1))
    const = lambda i, j: (0, 0)

    def conv_row0(i, j):
        return jnp.minimum(i, n_i - 1) * tm + j * rb

    def out_tile(i, j):
        return (jnp.maximum(i - 1, 0), jnp.where(i == 0, 0, j))

    return pl.pallas_call(
        functools.partial(_conv_out_kernel, n_row_tiles=n_i, seq=seq),
        grid=(n_i + 1, n_j),
        in_specs=[pl.BlockSpec((halo, width),
                               lambda i, j: (jnp.maximum(conv_row0(i, j) // halo - 1, 0), 0)),
                  pl.BlockSpec((rb, width), lambda i, j: (conv_row0(i, j) // rb, 0)),
                  pl.BlockSpec((halo, width),
                               lambda i, j: (jnp.minimum((conv_row0(i, j) + rb) // halo,
                                                         t // halo - 1), 0)),
                  pl.BlockSpec((rb, width), lambda i, j: (conv_row0(i, j) // rb, 1)),
                  pl.BlockSpec((CONV_TAPS, width), const, **once),
                  pl.BlockSpec((1, width), const, **once),
                  pl.BlockSpec((1, width), const, **once),
                  pl.BlockSpec((1, width), const, **once),
                  pl.BlockSpec((width, width), const, **once),
                  pl.BlockSpec((1, width), const, **once),
                  pl.BlockSpec((tm, width), lambda i, j: (jnp.maximum(i - 1, 0), 0)),
                  pl.BlockSpec((width, tn), lambda i, j: (0, j)),
                  pl.BlockSpec((width, tn), lambda i, j: (1, j)),
                  pl.BlockSpec((tm, tn), out_tile)],
        out_specs=pl.BlockSpec((tm, tn), out_tile),
        out_shape=jax.ShapeDtypeStruct((t, d), F32),
        scratch_shapes=[pltpu.VMEM((tm, width), BF16),
                        pltpu.VMEM((tm, width), BF16),
                        pltpu.VMEM((rb + 2 * halo, width), F32),
                        pltpu.VMEM((rb, width), F32)],
        compiler_params=_params("arbitrary", "arbitrary"),
        name="conv_out_proj",
    )(hglu, hglu, hglu, gates, dw_w, row(dw_b), row(ln_g), row(ln_b), pw_w, row(pw_b),
      attn, w_out, w_out, x2d)


def _layer(x, norm_gain, w_in, q_gain, k_gain, sink, dw_w, dw_b, ln_g, ln_b, pw_w, pw_b, w_out):
    batch, seq, d = x.shape
    t = batch * seq
    qw = N_Q_HEADS * HEAD_DIM
    kvw = N_KV_HEADS * HEAD_DIM
    cw = dw_w.shape[1]
    x2d = x.reshape(t, d)

    c_v = qw + kvw
    c_ga = c_v + kvw
    c_ua = c_ga + qw
    c_ub = c_ua + cw
    c_gc = c_ub + cw
    w_bf = w_in.astype(BF16)

    cos_t, sa_t, sb_t = _rope_tables(seq)
    qg = (q_gain.astype(F32) * (LOG2E / math.sqrt(HEAD_DIM))).reshape(1, HEAD_DIM)
    kg = k_gain.astype(F32).reshape(1, HEAD_DIM)
    xn, kv = _norm_kv_proj(x2d, norm_gain, w_bf, qw // (2 * kvw), cos_t, sa_t, sb_t, kg, seq)

    tm = 1024
    rope_spec = pl.BlockSpec((tm, HEAD_DIM), lambda i, j: (i % (seq // tm), 0))
    gain_spec = pl.BlockSpec((1, HEAD_DIM), lambda i, j: (0, 0))
    tn_q = 1024
    q = _proj_call(_q_kernel, xn, w_bf, [lambda j: j], [cos_t, sa_t, sb_t, qg],
                   [rope_spec] * 3 + [gain_spec], tm, tn_q, qw // tn_q, "proj_q")
    tn_g = 1024
    n_ga = qw // tn_g
    gates = _proj_call(_silu_kernel, xn, w_bf,
                       [lambda j: jnp.where(j < n_ga, c_ga // tn_g + j, c_gc // tn_g + j - n_ga)],
                       [], [], tm, tn_g, (qw + cw) // tn_g, "proj_gates")
    hglu, attn = _glu_attention(xn, w_bf, c_ua, c_ub, q, kv, gates, sink.astype(F32), seq)
    out = _conv_out_proj(hglu, gates, attn, x2d, dw_w, dw_b, ln_g, ln_b, pw_w.astype(BF16), pw_b,
                         w_out.astype(BF16), seq)
    return out.reshape(batch, seq, d)


def kernel(x, norm_gain, w_in, q_norm_gain, k_norm_gain, attn_sink, conv_dw_w, conv_dw_b,
           conv_ln_gain, conv_ln_bias, conv_pw_w, conv_pw_b, w_out):
    depth = norm_gain.shape[0]
    for layer in range(depth):
        x = _layer(x, norm_gain[layer], w_in[layer], q_norm_gain[layer], k_norm_gain[layer],
                   attn_sink[layer], conv_dw_w[layer], conv_dw_b[layer], conv_ln_gain[layer],
                   conv_ln_bias[layer], conv_pw_w[layer], conv_pw_b[layer], w_out[layer])
    return x
```
